```python
import jax, jax.numpy as jnp
from jax import lax
import numpy as np


D_MODEL = 1024
BATCH = 2
SEQ = 16384
DEPTH = 2

D_MIX = D_MODEL
D_POOL = D_MIX // 2
POOL_WINDOWS = (2, 4, 8, 16)
N_POOL_GROUPS = len(POOL_WINDOWS)
POOL_GROUP_DIM = D_POOL // N_POOL_GROUPS
D_ATTN = D_MIX - D_POOL
NA_HEAD_DIM = 32
NA_HEADS = D_ATTN // NA_HEAD_DIM
D_IN = D_POOL + 3 * D_ATTN
GRID_W = 64
NA_KH_MAX = 8
NA_KW = 16
D_FF = 4 * D_MODEL
N_EXPERTS = 8
TOP_K = 2
D_FF_EXPERT = (7 * D_MODEL) // 2
MOE_BLOCK = 256
RMS_EPS = 1e-6
N_DENSE = (DEPTH + 1) // 2
N_MOE = DEPTH // 2

kernel_name = 'hybrid_pool_natten_moe_encoder'


def rmsnorm(x, g):
    xf = x.astype(jnp.float32)
    y = xf * lax.rsqrt(jnp.mean(xf * xf, axis=-1, keepdims=True) + RMS_EPS)
    return (y * g.astype(jnp.float32)).astype(x.dtype)


def pool_mixer(u, pool_w, pool_scale):
    S = u.shape[1]
    uf = u.astype(jnp.float32)
    cs = jnp.concatenate([jnp.zeros_like(uf[:, :1]), jnp.cumsum(uf, axis=1)], axis=1)
    t = np.arange(S)
    outs = []
    for g, w in enumerate(POOL_WINDOWS):
        c_lo, c_hi = g * POOL_GROUP_DIM, (g + 1) * POOL_GROUP_DIM
        lo = np.clip(t - w // 2, 0, S)
        hi = np.clip(t + w - w // 2, 0, S)
        csg = cs[:, :, c_lo:c_hi]
        win_mean = (csg[:, hi] - csg[:, lo]) / (hi - lo).astype(np.float32)[None, :, None]
        delta = (win_mean - uf[:, :, c_lo:c_hi]).astype(u.dtype)
        outs.append(jnp.einsum('bsc,cd->bsd', delta, pool_w[g]))
    return jnp.concatenate(outs, axis=-1) * pool_scale


def neighbourhood_attention(q, k, v, rpb):
    B, S = q.shape[0], q.shape[1]
    rows = S // GRID_W
    kh = min(NA_KH_MAX, rows)
    qg = (q * (NA_HEAD_DIM ** -0.5)).reshape(B, rows, GRID_W, NA_HEADS, NA_HEAD_DIM)
    kg = k.reshape(B, rows, GRID_W, NA_HEADS, NA_HEAD_DIM)
    vg = v.reshape(B, rows, GRID_W, NA_HEADS, NA_HEAD_DIM)
    cols = np.arange(GRID_W)
    c0 = np.clip(cols - NA_KW // 2, 0, GRID_W - NA_KW)
    col_idx = c0[:, None] + np.arange(NA_KW)[None, :]
    dc = col_idx - cols[:, None] + (NA_KW - 1)

    def one_row(r):
        r0 = jnp.clip(r - kh // 2, 0, rows - kh)
        k_rows = lax.dynamic_slice_in_dim(kg, r0, kh, axis=1)
        v_rows = lax.dynamic_slice_in_dim(vg, r0, kh, axis=1)
        k_nb = k_rows[:, :, col_idx]
        v_nb = v_rows[:, :, col_idx]
        q_row = lax.dynamic_index_in_dim(qg, r, axis=1, keepdims=False)
        s = jnp.einsum('bchd,bicjhd->bhcij', q_row, k_nb, preferred_element_type=jnp.float32)
        dr = r0 + jnp.arange(kh) - r + (NA_KH_MAX - 1)
        bias = rpb[:, dr[:, None, None], dc[None, :, :]]
        s = s + jnp.transpose(bias, (0, 2, 1, 3)).astype(jnp.float32)[None]
        p = jax.nn.softmax(s.reshape(B, NA_HEADS, GRID_W, kh * NA_KW), axis=-1)
        p = p.reshape(B, NA_HEADS, GRID_W, kh, NA_KW).astype(v.dtype)
        return jnp.einsum('bhcij,bicjhd->bchd', p, v_nb)

    out = lax.map(one_row, jnp.arange(rows))
    return jnp.transpose(out, (1, 0, 2, 3, 4)).reshape(B, S, D_ATTN)


def swiglu(x, wg, wu, wd):
    a = jnp.einsum('bsd,df->bsf', x, wg)
    b = jnp.einsum('bsd,df->bsf', x, wu)
    return jnp.einsum('bsf,fd->bsd', jax.nn.silu(a) * b, wd)


def moe_swiglu(h, w_router, w_gate, w_up, w_down):
    B, S, D = h.shape
    T = B * S
    xt = h.reshape(T, D)
    logits = jnp.einsum('td,de->te', xt, w_router, preferred_element_type=jnp.float32)
    top_logit, top_idx = lax.top_k(logits, TOP_K)
    gates = jax.nn.softmax(top_logit, axis=-1)
    A = T * TOP_K
    e_flat = top_idx.reshape(A)
    g_flat = gates.reshape(A)
    tok_flat = jnp.repeat(jnp.arange(T, dtype=jnp.int32), TOP_K)
    order = jnp.argsort(e_flat)
    e_sorted = e_flat[order]
    counts = jnp.bincount(e_flat, length=N_EXPERTS)
    padded = (counts + MOE_BLOCK - 1) // MOE_BLOCK * MOE_BLOCK
    start = jnp.cumsum(counts) - counts
    pad_end = jnp.cumsum(padded)
    pad_start = pad_end - padded
    dest = pad_start[e_sorted] + (jnp.arange(A) - start[e_sorted])
    n_blocks = (A + MOE_BLOCK - 1) // MOE_BLOCK + N_EXPERTS
    n_rows = n_blocks * MOE_BLOCK
    row_tok = jnp.zeros((n_rows,), jnp.int32).at[dest].set(tok_flat[order])
    row_gate = jnp.zeros((n_rows,), jnp.float32).at[dest].set(g_flat[order])
    block_expert = jnp.minimum(
        jnp.searchsorted(pad_end, jnp.arange(n_blocks) * MOE_BLOCK, side='right'), N_EXPERTS - 1)

    def run_block(args):
        tok, gate, e = args
        xb = xt[tok]
        a = xb @ w_gate[e]
        b = xb @ w_up[e]
        y = (jax.nn.silu(a) * b) @ w_down[e]
        return y * gate[:, None].astype(y.dtype)

    ys = lax.map(run_block, (row_tok.reshape(n_blocks, MOE_BLOCK),
                             row_gate.reshape(n_blocks, MOE_BLOCK), block_expert))
    out = jnp.zeros((T, D), h.dtype).at[row_tok].add(ys.reshape(n_rows, D).astype(h.dtype))
    return out.reshape(B, S, D)


def setup_inputs(seed: int = 0) -> dict:
    key = jax.random.key(seed)
    ks = jax.random.split(key, 17)
    f32 = jnp.float32

    def nrm(k, shape, scale):
        return jax.random.normal(k, shape, f32) * scale

    def gain(k, shape):
        return 1.0 + 0.05 * jax.random.normal(k, shape, f32)

    return {
        'x': nrm(ks[0], (BATCH, SEQ, D_MODEL), 1.0),
        'mix_norm_pre': gain(ks[1], (DEPTH, D_MODEL)),
        'mix_norm_post': gain(ks[2], (DEPTH, D_MODEL)),
        'ffn_norm_pre': gain(ks[3], (DEPTH, D_MODEL)),
        'ffn_norm_post': gain(ks[4], (DEPTH, D_MODEL)),
        'w_in': nrm(ks[5], (DEPTH, D_MODEL, D_IN), D_MODEL ** -0.5),
        'pool_w': nrm(ks[6], (DEPTH, N_POOL_GROUPS, POOL_GROUP_DIM, POOL_GROUP_DIM), POOL_GROUP_DIM ** -0.5),
        'pool_scale': gain(ks[7], (DEPTH, D_POOL)),
        'na_rpb': nrm(ks[8], (DEPTH, NA_HEADS, 2 * NA_KH_MAX - 1, 2 * NA_KW - 1), 0.1),
        'w_out': nrm(ks[9], (DEPTH, D_MIX, D_MODEL), D_MIX ** -0.5),
        'dense_w_gate': nrm(ks[10], (N_DENSE, D_MODEL, D_FF), D_MODEL ** -0.5),
        'dense_w_up': nrm(ks[11], (N_DENSE, D_MODEL, D_FF), D_MODEL ** -0.5),
        'dense_w_down': nrm(ks[12], (N_DENSE, D_FF, D_MODEL), D_FF ** -0.5),
        'moe_router': nrm(ks[13], (N_MOE, D_MODEL, N_EXPERTS), D_MODEL ** -0.5),
        'moe_w_gate': nrm(ks[14], (N_MOE, N_EXPERTS, D_MODEL, D_FF_EXPERT), D_MODEL ** -0.5),
        'moe_w_up': nrm(ks[15], (N_MOE, N_EXPERTS, D_MODEL, D_FF_EXPERT), D_MODEL ** -0.5),
        'moe_w_down': nrm(ks[16], (N_MOE, N_EXPERTS, D_FF_EXPERT, D_MODEL), D_FF_EXPERT ** -0.5),
    }


def reference(x, mix_norm_pre, mix_norm_post, ffn_norm_pre, ffn_norm_post, w_in, pool_w,
              pool_scale, na_rpb, w_out, dense_w_gate, dense_w_up, dense_w_down,
              moe_router, moe_w_gate, moe_w_up, moe_w_down):
    B, S, _ = x.shape
    h = x
    for layer in range(DEPTH):
        hn = rmsnorm(h, mix_norm_pre[layer])
        proj = jnp.einsum('bsd,de->bse', hn, w_in[layer])
        u = proj[..., :D_POOL]
        q = proj[..., D_POOL:D_POOL + D_ATTN].reshape(B, S, NA_HEADS, NA_HEAD_DIM)
        k = proj[..., D_POOL + D_ATTN:D_POOL + 2 * D_ATTN].reshape(B, S, NA_HEADS, NA_HEAD_DIM)
        v = proj[..., D_POOL + 2 * D_ATTN:].reshape(B, S, NA_HEADS, NA_HEAD_DIM)
        y_pool = pool_mixer(u, pool_w[layer], pool_scale[layer])
        y_na = neighbourhood_attention(q, k, v, na_rpb[layer])
        mix = jnp.einsum('bse,ed->bsd', jnp.concatenate([y_pool, y_na], axis=-1), w_out[layer])
        h = h + rmsnorm(mix, mix_norm_post[layer])
        hn = rmsnorm(h, ffn_norm_pre[layer])
        j = layer // 2
        if layer % 2 == 0:
            f = swiglu(hn, dense_w_gate[j], dense_w_up[j], dense_w_down[j])
        else:
            f = moe_swiglu(hn, moe_router[j], moe_w_gate[j], moe_w_up[j], moe_w_down[j])
        h = h + rmsnorm(f, ffn_norm_post[layer])
    return h
```

```python
import functools

import jax
import jax.numpy as jnp
import numpy as np
from jax import lax
from jax.experimental import pallas as pl
from jax.experimental.pallas import tpu as pltpu

F32 = jnp.float32
BF16 = jnp.bfloat16

D_MODEL = 1024
D_POOL = 512
POOL_WINDOWS = (2, 4, 8, 16)
POOL_GROUP_DIM = 128
D_ATTN = 512
NA_HEAD_DIM = 32
NA_HEADS = 16
D_IN = D_POOL + 3 * D_ATTN
GRID_W = 64
NA_KH = 8
NA_KW = 16
N_EXPERTS = 8
TOP_K = 2
RMS_EPS = 1e-6
NEG_BIG = -1e30

LANES = 128
HEADS_PER_GROUP = 256 // NA_HEAD_DIM
VMEM_LIMIT = 52 * 1024 * 1024

ROW_TILE = 512
POOL_SUB = 128
POOL_HALO = 64
FFN_ROWS = 512
FFN_COLS = 512
MOE_TOK_TILE = 256


def _rms(x, g):
    ms = jnp.mean(x * x, axis=-1, keepdims=True)
    return x * lax.rsqrt(ms + RMS_EPS) * g


def _params(sem):
    return pltpu.CompilerParams(dimension_semantics=sem, vmem_limit_bytes=VMEM_LIMIT)


def _norm_proj_kernel(x_ref, g_ref, w_ref, o_ref):
    hn = _rms(x_ref[...], g_ref[...])
    p = jnp.dot(hn.astype(BF16), w_ref[...], preferred_element_type=F32)
    q_lo, q_hi = D_POOL, D_POOL + D_ATTN
    o_ref[:, :q_lo] = p[:, :q_lo].astype(BF16)
    o_ref[:, q_lo:q_hi] = (p[:, q_lo:q_hi] * (NA_HEAD_DIM ** -0.5)).astype(BF16)
    o_ref[:, q_hi:] = p[:, q_hi:].astype(BF16)


def _norm_proj(h, g, w_bf16):
    t = h.shape[0]
    tm = min(ROW_TILE, t)
    return pl.pallas_call(
        _norm_proj_kernel,
        grid=(t // tm,),
        in_specs=[
            pl.BlockSpec((tm, D_MODEL), lambda i: (i, 0)),
            pl.BlockSpec((1, D_MODEL), lambda i: (0, 0)),
            pl.BlockSpec((D_MODEL, D_IN), lambda i: (0, 0)),
        ],
        out_specs=pl.BlockSpec((tm, D_IN), lambda i: (i, 0)),
        out_shape=jax.ShapeDtypeStruct((t, D_IN), BF16),
        compiler_params=_params(("parallel",)),
        name="norm_proj",
    )(h, g.reshape(1, D_MODEL), w_bf16)


def _pool_kernel(seq_len, ts, cur_ref, prev_ref, next_ref, pw_ref, ps_ref, o_ref, win_ref):
    t0 = pl.program_id(1) * ts
    win_ref[:POOL_HALO, :] = prev_ref[0]
    win_ref[POOL_HALO:POOL_HALO + ts, :] = cur_ref[0]
    win_ref[POOL_HALO + ts:, :] = next_ref[0]
    kdim = POOL_SUB + 2 * POOL_HALO
    row = lax.broadcasted_iota(jnp.int32, (POOL_SUB, kdim), 0)
    col = lax.broadcasted_iota(jnp.int32, (POOL_SUB, kdim), 1)
    rel = col - row - POOL_HALO
    trow = lax.broadcasted_iota(jnp.int32, (POOL_SUB, 1), 0)
    for s in range(ts // POOL_SUB):
        base = t0 + s * POOL_SUB
        tok = base - POOL_HALO + col
        in_seq = (tok >= 0) & (tok < seq_len)
        t_abs = base + trow
        for g, w in enumerate(POOL_WINDOWS):
            half = w // 2
            band = jnp.where((rel >= -half) & (rel < w - half) & in_seq, 1.0, 0.0).astype(BF16)
            lo = jnp.maximum(t_abs - half, 0)
            hi = jnp.minimum(t_abs + (w - half), seq_len)
            cnt = (hi - lo).astype(F32)
            c0, c1 = g * POOL_GROUP_DIM, (g + 1) * POOL_GROUP_DIM
            uwin = win_ref[s * POOL_SUB:s * POOL_SUB + kdim, c0:c1]
            wsum = jnp.dot(band, uwin, preferred_element_type=F32)
            u = cur_ref[0, s * POOL_SUB:(s + 1) * POOL_SUB, c0:c1].astype(F32)
            delta = wsum / cnt - u
            y = jnp.dot(delta.astype(BF16), pw_ref[g], preferred_element_type=F32)
            o_ref[0, s * POOL_SUB:(s + 1) * POOL_SUB, c0:c1] = (y * ps_ref[:, c0:c1]).astype(BF16)


def _pool_mixer(proj3, pool_w_bf16, pool_scale):
    b, s, _ = proj3.shape
    ts = min(ROW_TILE, s)
    hb = ts // POOL_HALO
    n_halo = s // POOL_HALO
    return pl.pallas_call(
        functools.partial(_pool_kernel, s, ts),
        grid=(b, s // ts),
        in_specs=[
            pl.BlockSpec((1, ts, D_POOL), lambda bi, i: (bi, i, 0)),
            pl.BlockSpec((1, POOL_HALO, D_POOL), lambda bi, i: (bi, jnp.maximum(i * hb - 1, 0), 0)),
            pl.BlockSpec((1, POOL_HALO, D_POOL),
                         lambda bi, i: (bi, jnp.minimum((i + 1) * hb, n_halo - 1), 0)),
            pl.BlockSpec((len(POOL_WINDOWS), POOL_GROUP_DIM, POOL_GROUP_DIM), lambda bi, i: (0, 0, 0)),
            pl.BlockSpec((1, D_POOL), lambda bi, i: (0, 0)),
        ],
        out_specs=pl.BlockSpec((1, ts, D_POOL), lambda bi, i: (bi, i, 0)),
        out_shape=jax.ShapeDtypeStruct((b, s, D_POOL), BF16),
        scratch_shapes=[pltpu.VMEM((ts + 2 * POOL_HALO, D_POOL), BF16)],
        compiler_params=_params(("parallel", "parallel")),
        name="pool_mixer",
    )(proj3, proj3, proj3, pool_w_bf16, pool_scale.reshape(1, D_POOL))


def _na_bias_table(rpb):
    cols = np.arange(GRID_W)
    c0 = np.clip(cols - NA_KW // 2, 0, GRID_W - NA_KW)
    kc = np.arange(GRID_W)
    inside = (kc[None, :] >= c0[:, None]) & (kc[None, :] < c0[:, None] + NA_KW)
    dc = np.clip(kc[None, :] - cols[:, None] + (NA_KW - 1), 0, 2 * NA_KW - 2)
    dr = np.arange(NA_KH)[:, None] + np.arange(NA_KH)[None, :]
    tab = rpb.astype(F32)[:, dr[:, :, None, None], dc[None, None, :, :]]
    tab = jnp.where(inside[None, None, None], tab, NEG_BIG)
    tab = jnp.transpose(tab, (1, 0, 3, 2, 4))
    return tab.reshape(NA_KH, NA_HEADS, GRID_W, NA_KH * GRID_W)


def _na_kernel(q_ref, *refs):
    k_refs = refs[:NA_KH]
    v_refs = refs[NA_KH:2 * NA_KH]
    bias_ref = refs[2 * NA_KH]
    o_ref = refs[2 * NA_KH + 1]
    gw = HEADS_PER_GROUP * NA_HEAD_DIM
    nkeys = NA_KH * GRID_W
    lane_head = lax.broadcasted_iota(jnp.int32, (HEADS_PER_GROUP, 1, gw), 2) // NA_HEAD_DIM
    head_id = lax.broadcasted_iota(jnp.int32, (HEADS_PER_GROUP, 1, gw), 0)
    own = lane_head == head_id
    for g in range(NA_HEADS // HEADS_PER_GROUP):
        lo, hi = g * gw, (g + 1) * gw
        qg = q_ref[0, 0, :, lo:hi]
        zero = jnp.zeros_like(qg)
        qm = jnp.where(own, qg[None], zero[None]).reshape(HEADS_PER_GROUP * GRID_W, gw)
        kw = jnp.concatenate([r[0, 0, :, lo:hi] for r in k_refs], axis=0)
        vw = jnp.concatenate([r[0, 0, :, lo:hi] for r in v_refs], axis=0)
        s = lax.dot_general(qm, kw, (((1,), (1,)), ((), ())), preferred_element_type=F32)
        s = s + bias_ref[0, g * HEADS_PER_GROUP:(g + 1) * HEADS_PER_GROUP].reshape(
            HEADS_PER_GROUP * GRID_W, nkeys)
        m = jnp.max(s, axis=-1, keepdims=True)
        p = jnp.exp(s - m)
        l = jnp.sum(p, axis=-1, keepdims=True)
        pv = jnp.dot(p.astype(BF16), vw, preferred_element_type=F32)
        pv = (pv / l).reshape(HEADS_PER_GROUP, GRID_W, gw)
        out = jnp.sum(jnp.where(own, pv, 0.0), axis=0)
        o_ref[0, 0, :, lo:hi] = out.astype(BF16)


def _neighbourhood_attention(proj4, bias_tab):
    b, rows, w, _ = proj4.shape
    assert w == GRID_W and rows >= NA_KH

    def r0(r):
        return jnp.clip(r - NA_KH // 2, 0, rows - NA_KH)

    def kv_spec(i, col_block):
        return pl.BlockSpec((1, 1, GRID_W, D_ATTN), lambda bi, r: (bi, r0(r) + i, 0, col_block))

    in_specs = [pl.BlockSpec((1, 1, GRID_W, D_ATTN), lambda bi, r: (bi, r, 0, 1))]
    in_specs += [kv_spec(i, 2) for i in range(NA_KH)]
    in_specs += [kv_spec(i, 3) for i in range(NA_KH)]
    in_specs += [pl.BlockSpec((1, NA_HEADS, GRID_W, NA_KH * GRID_W),
                              lambda bi, r: (r0(r) - r + NA_KH - 1, 0, 0, 0))]
    return pl.pallas_call(
        _na_kernel,
        grid=(b, rows),
        in_specs=in_specs,
        out_specs=pl.BlockSpec((1, 1, GRID_W, D_ATTN), lambda bi, r: (bi, r, 0, 0)),
        out_shape=jax.ShapeDtypeStruct((b, rows, GRID_W, D_ATTN), BF16),
        compiler_params=_params(("parallel", "arbitrary")),
        name="neighbourhood_attention",
    )(*([proj4] * (1 + 2 * NA_KH)), bias_tab)


def _out_proj_kernel(with_router, yp_ref, ya_ref, w_ref, h_ref, gpost_ref, gpre_ref, *refs):
    mix = jnp.dot(yp_ref[...], w_ref[:D_POOL, :], preferred_element_type=F32)
    mix = mix + jnp.dot(ya_ref[...], w_ref[D_POOL:, :], preferred_element_type=F32)
    h1 = h_ref[...] + _rms(mix, gpost_ref[...])
    hn = _rms(h1, gpre_ref[...])
    if with_router:
        wr_ref, h1_ref, hn_ref, logit_ref = refs
        hn_ref[...] = hn
        logit_ref[...] = jnp.dot(hn.astype(BF16), wr_ref[...], preferred_element_type=F32)
    else:
        h1_ref, hn_ref = refs
        hn_ref[...] = hn.astype(BF16)
    h1_ref[...] = h1


def _out_proj(y_pool, y_na, w_out_bf16, h, g_post, g_pre, router_bf16=None):
    t = h.shape[0]
    tm = min(ROW_TILE, t)
    with_router = router_bf16 is not None
    row = lambda width: pl.BlockSpec((tm, width), lambda i: (i, 0))
    full = lambda a, b: pl.BlockSpec((a, b), lambda i: (0, 0))
    in_specs = [row(D_POOL), row(D_ATTN), full(D_MODEL, D_MODEL), row(D_MODEL),
                full(1, D_MODEL), full(1, D_MODEL)]
    args = [y_pool, y_na, w_out_bf16, h, g_post.reshape(1, D_MODEL), g_pre.reshape(1, D_MODEL)]
    out_specs = [row(D_MODEL), row(D_MODEL)]
    out_shape = [jax.ShapeDtypeStruct((t, D_MODEL), F32),
                 jax.ShapeDtypeStruct((t, D_MODEL), F32 if with_router else BF16)]
    if with_router:
        in_specs.append(full(D_MODEL, LANES))
        args.append(router_bf16)
        out_specs.append(row(LANES))
        out_shape.append(jax.ShapeDtypeStruct((t, LANES), F32))
    return pl.pallas_call(
        functools.partial(_out_proj_kernel, with_router),
        grid=(t // tm,),
        in_specs=in_specs,
        out_specs=out_specs,
        out_shape=out_shape,
        compiler_params=_params(("parallel",)),
        name="out_proj_router" if with_router else "out_proj",
    )(*args)


def _swiglu_kernel(fuse_norm, be_ref, nused_ref, x_ref, wg_ref, wu_ref, wd_ref, *refs):
    if fuse_norm:
        h_ref, g_ref, o_ref, acc_ref = refs
    else:
        o_ref, acc_ref = refs
    j = pl.program_id(0)
    f = pl.program_id(1)
    live = j < nused_ref[0]

    @pl.when(live)
    def _():
        x = x_ref[...].astype(BF16)
        a = jnp.dot(x, wg_ref[0], preferred_element_type=F32)
        b = jnp.dot(x, wu_ref[0], preferred_element_type=F32)
        hmid = (a * jax.nn.sigmoid(a) * b).astype(BF16)
        part = jnp.dot(hmid, wd_ref[0], preferred_element_type=F32)

        @pl.when(f == 0)
        def _():
            acc_ref[...] = part

        @pl.when(f > 0)
        def _():
            acc_ref[...] += part

    @pl.when(f == pl.num_programs(1) - 1)
    def _():
        @pl.when(live)
        def _():
            if fuse_norm:
                o_ref[...] = h_ref[...] + _rms(acc_ref[...], g_ref[...])
            else:
                o_ref[...] = acc_ref[...]

        @pl.when(jnp.logical_not(live))
        def _():
            o_ref[...] = jnp.zeros_like(o_ref)


def _swiglu(x, wg, wu, wd, block_expert, n_used, resid=None, g_post=None):
    rows = x.shape[0]
    dff = wg.shape[-1]
    bm = min(FFN_ROWS, rows)
    tf = min(FFN_COLS, dff)
    assert rows % bm == 0 and dff % tf == 0
    nf = dff // tf
    fuse_norm = resid is not None

    def f_eff(j, f, nu):
        return jnp.where(j < nu[0], f, nf - 1)

    in_specs = [
        pl.BlockSpec((bm, D_MODEL), lambda j, f, be, nu: (j, 0)),
        pl.BlockSpec((1, D_MODEL, tf), lambda j, f, be, nu: (be[j], 0, f_eff(j, f, nu))),
        pl.BlockSpec((1, D_MODEL, tf), lambda j, f, be, nu: (be[j], 0, f_eff(j, f, nu))),
        pl.BlockSpec((1, tf, D_MODEL), lambda j, f, be, nu: (be[j], f_eff(j, f, nu), 0)),
    ]
    args = [x, wg, wu, wd]
    if fuse_norm:
        in_specs += [pl.BlockSpec((bm, D_MODEL), lambda j, f, be, nu: (j, 0)),
                     pl.BlockSpec((1, D_MODEL), lambda j, f, be, nu: (0, 0))]
        args += [resid, g_post.reshape(1, D_MODEL)]
    return pl.pallas_call(
        functools.partial(_swiglu_kernel, fuse_norm),
        grid_spec=pltpu.PrefetchScalarGridSpec(
            num_scalar_prefetch=2,
            grid=(rows // bm, nf),
            in_specs=in_specs,
            out_specs=pl.BlockSpec((bm, D_MODEL), lambda j, f, be, nu: (j, 0)),
            scratch_shapes=[pltpu.VMEM((bm, D_MODEL), F32)],
        ),
        out_shape=jax.ShapeDtypeStruct((rows, D_MODEL), F32),
        compiler_params=_params(("parallel", "arbitrary")),
        name="swiglu_dense" if fuse_norm else "swiglu_experts",
    )(block_expert, n_used, *args)


def _dispatch_kernel(dest_ref, hn_ref, xs_in_ref, xs_ref, sem):
    del xs_in_ref
    n_tok = hn_ref.shape[0]

    def copy(i, k):
        return pltpu.make_async_copy(hn_ref.at[pl.ds(i, 1)],
                                     xs_ref.at[pl.ds(dest_ref[0, 0, TOP_K * i + k], 1)], sem)

    def start(i, c):
        for k in range(TOP_K):
            copy(i, k).start()
        return c

    def wait(i, c):
        for k in range(TOP_K):
            copy(i, k).wait()
        return c

    lax.fori_loop(0, n_tok, start, 0)
    lax.fori_loop(0, n_tok, wait, 0)


def _dispatch(hn, dest, n_rows):
    t = hn.shape[0]
    tb = min(MOE_TOK_TILE, t)
    xs0 = jnp.zeros((n_rows, D_MODEL), F32)
    return pl.pallas_call(
        _dispatch_kernel,
        grid=(t // tb,),
        in_specs=[
            pl.BlockSpec((1, 1, TOP_K * tb), lambda i: (i, 0, 0), memory_space=pltpu.SMEM),
            pl.BlockSpec((tb, D_MODEL), lambda i: (i, 0)),
            pl.BlockSpec(memory_space=pl.ANY),
        ],
        out_specs=pl.BlockSpec(memory_space=pl.ANY),
        out_shape=jax.ShapeDtypeStruct((n_rows, D_MODEL), F32),
        scratch_shapes=[pltpu.SemaphoreType.DMA(())],
        input_output_aliases={2: 0},
        compiler_params=_params(("arbitrary",)),
        name="moe_dispatch",
    )(dest.reshape(t // tb, 1, TOP_K * tb), hn, xs0)


def _combine_kernel(dest_ref, ys_ref, gate_ref, h_ref, g_ref, o_ref, buf_ref, sem):
    n_tok = h_ref.shape[0]

    def copy(i, k):
        return pltpu.make_async_copy(ys_ref.at[pl.ds(dest_ref[0, 0, TOP_K * i + k], 1)],
                                     buf_ref.at[k, pl.ds(i, 1)], sem)

    def start(i, c):
        for k in range(TOP_K):
            copy(i, k).start()
        return c

    def wait(i, c):
        for k in range(TOP_K):
            copy(i, k).wait()
        return c

    lax.fori_loop(0, n_tok, start, 0)
    lax.fori_loop(0, n_tok, wait, 0)
    f = buf_ref[0] * gate_ref[:, 0:1]
    for k in range(1, TOP_K):
        f = f + buf_ref[k] * gate_ref[:, k:k + 1]
    o_ref[...] = h_ref[...] + _rms(f, g_ref[...])


def _combine(ys, dest, gates, h, g_post):
    t = h.shape[0]
    tb = min(MOE_TOK_TILE, t)
    return pl.pallas_call(
        _combine_kernel,
        grid=(t // tb,),
        in_specs=[
            pl.BlockSpec((1, 1, TOP_K * tb), lambda i: (i, 0, 0), memory_space=pltpu.SMEM),
            pl.BlockSpec(memory_space=pl.ANY),
            pl.BlockSpec((tb, TOP_K), lambda i: (i, 0)),
            pl.BlockSpec((tb, D_MODEL), lambda i: (i, 0)),
            pl.BlockSpec((1, D_MODEL), lambda i: (0, 0)),
        ],
        out_specs=pl.BlockSpec((tb, D_MODEL), lambda i: (i, 0)),
        out_shape=jax.ShapeDtypeStruct((t, D_MODEL), F32),
        scratch_shapes=[pltpu.VMEM((TOP_K, tb, D_MODEL), F32), pltpu.SemaphoreType.DMA(())],
        compiler_params=_params(("arbitrary",)),
        name="moe_combine",
    )(dest.reshape(t // tb, 1, TOP_K * tb), ys, gates, h, g_post.reshape(1, D_MODEL))


def _route(logits, bm):
    t = logits.shape[0]
    lane = lax.broadcasted_iota(jnp.int32, logits.shape, 1)
    i1 = jnp.argmax(logits, axis=-1).astype(jnp.int32)
    l1 = jnp.max(logits, axis=-1)
    rest = jnp.where(lane == i1[:, None], -jnp.inf, logits)
    i2 = jnp.argmax(rest, axis=-1).astype(jnp.int32)
    l2 = jnp.max(rest, axis=-1)
    gates = jax.nn.softmax(jnp.stack([l1, l2], axis=-1), axis=-1)
    e_flat = jnp.stack([i1, i2], axis=-1).reshape(t * TOP_K)
    onehot = (e_flat[:, None] == jnp.arange(N_EXPERTS, dtype=jnp.int32)[None, :]).astype(jnp.int32)
    csum = jnp.cumsum(onehot, axis=0)
    rank = jnp.sum((csum - onehot) * onehot, axis=-1)
    counts = csum[-1]
    padded = (counts + bm - 1) // bm * bm
    pad_end = jnp.cumsum(padded)
    pad_start = pad_end - padded
    dest = (pad_start[e_flat] + rank).astype(jnp.int32)
    n_blocks = (t * TOP_K) // bm + N_EXPERTS
    n_used = (pad_end[-1] // bm).astype(jnp.int32)
    blk = jnp.minimum(jnp.arange(n_blocks, dtype=jnp.int32), n_used - 1) * bm
    block_expert = jnp.minimum(jnp.searchsorted(pad_end, blk, side="right"), N_EXPERTS - 1)
    return dest, gates, block_expert.astype(jnp.int32), n_used.reshape(1), n_blocks * bm


def kernel(x, mix_norm_pre, mix_norm_post, ffn_norm_pre, ffn_norm_post, w_in, pool_w, pool_scale, na_rpb,
           w_out, dense_w_gate, dense_w_up, dense_w_down, moe_router, moe_w_gate, moe_w_up, moe_w_down):
    b, s, d = x.shape
    assert d == D_MODEL and s % GRID_W == 0
    t = b * s
    rows = s // GRID_W
    depth = w_in.shape[0]
    h = x.reshape(t, d)
    for layer in range(depth):
        j = layer // 2
        is_moe = layer % 2 == 1
        proj = _norm_proj(h, mix_norm_pre[layer], w_in[layer].astype(BF16))
        y_pool = _pool_mixer(proj.reshape(b, s, D_IN), pool_w[layer].astype(BF16), pool_scale[layer])
        y_na = _neighbourhood_attention(proj.reshape(b, rows, GRID_W, D_IN), _na_bias_table(na_rpb[layer]))
        router = None
        if is_moe:
            router = jnp.pad(moe_router[j], ((0, 0), (0, LANES - N_EXPERTS))).astype(BF16)
        outs = _out_proj(y_pool.reshape(t, D_POOL), y_na.reshape(t, D_ATTN), w_out[layer].astype(BF16), h,
                         mix_norm_post[layer], ffn_norm_pre[layer], router)
        if is_moe:
            h1, hn, logits = outs
            bm = min(FFN_ROWS, t)
            dest, gates, block_expert, n_used, n_rows = _route(logits[:, :N_EXPERTS], bm)
            xs = _dispatch(hn, dest, n_rows)
            ys = _swiglu(xs, moe_w_gate[j].astype(BF16), moe_w_up[j].astype(BF16),
                         moe_w_down[j].astype(BF16), block_expert, n_used)
            h = _combine(ys, dest, gates, h1, ffn_norm_post[layer])
        else:
            h1, hn = outs
            n_blk = t // min(FFN_ROWS, t)
            h = _swiglu(hn, dense_w_gate[j][None].astype(BF16), dense_w_up[j][None].astype(BF16),
                        dense_w_down[j][None].astype(BF16), jnp.zeros((n_blk,), jnp.int32),
                        jnp.full((1,), n_blk, jnp.int32), resid=h1, g_post=ffn_norm_post[layer])
    return h.reshape(b, s, d)
```

```python
import functools

import jax
import jax.numpy as jnp
import numpy as np
from jax import lax
from jax.experimental import pallas as pl
from jax.experimental.pallas import tpu as pltpu

F32 = jnp.float32
BF16 = jnp.bfloat16

D_MODEL = 1024
D_POOL = 512
POOL_WINDOWS = (2, 4, 8, 16)
POOL_GROUP_DIM = 128
D_ATTN = 512
NA_HEAD_DIM = 32
NA_HEADS = 16
D_IN = D_POOL + 3 * D_ATTN
GRID_W = 64
NA_KH = 8
NA_KW = 16
N_EXPERTS = 8
TOP_K = 2
RMS_EPS = 1e-6
NEG_BIG = -1e30
LOG2E = 1.4426950408889634

LANES = 128
HEADS_PER_GROUP = 256 // NA_HEAD_DIM
VMEM_LIMIT = 52 * 1024 * 1024

ROW_TILE = 512
POOL_SUB = 128
POOL_HALO = 64
FFN_ROWS = 512
FFN_COLS_MAX = 2048
MXU_TILE = 256
MOE_TOK_TILE = 256


def _rms(x, g):
    ms = jnp.mean(x * x, axis=-1, keepdims=True)
    return x * lax.rsqrt(ms + RMS_EPS) * g


def _params(sem):
    return pltpu.CompilerParams(dimension_semantics=sem, vmem_limit_bytes=VMEM_LIMIT)


def _norm_proj_kernel(x_ref, g_ref, w_ref, o_ref):
    hn = _rms(x_ref[...], g_ref[...])
    p = jnp.dot(hn.astype(BF16), w_ref[...], preferred_element_type=F32)
    q_lo, q_hi = D_POOL, D_POOL + D_ATTN
    o_ref[:, :q_lo] = p[:, :q_lo].astype(BF16)
    o_ref[:, q_lo:q_hi] = (p[:, q_lo:q_hi] * (NA_HEAD_DIM ** -0.5 * LOG2E)).astype(BF16)
    o_ref[:, q_hi:] = p[:, q_hi:].astype(BF16)


def _norm_proj(h, g, w_bf16):
    t = h.shape[0]
    tm = min(ROW_TILE, t)
    return pl.pallas_call(
        _norm_proj_kernel,
        grid=(t // tm,),
        in_specs=[
            pl.BlockSpec((tm, D_MODEL), lambda i: (i, 0)),
            pl.BlockSpec((1, D_MODEL), lambda i: (0, 0)),
            pl.BlockSpec((D_MODEL, D_IN), lambda i: (0, 0)),
        ],
        out_specs=pl.BlockSpec((tm, D_IN), lambda i: (i, 0)),
        out_shape=jax.ShapeDtypeStruct((t, D_IN), BF16),
        compiler_params=_params(("parallel",)),
        name="norm_proj",
    )(h, g.reshape(1, D_MODEL), w_bf16)


def _pool_kernel(seq_len, ts, cur_ref, prev_ref, next_ref, pw_ref, ps_ref, o_ref, win_ref):
    t0 = pl.program_id(1) * ts
    win_ref[:POOL_HALO, :] = prev_ref[0]
    win_ref[POOL_HALO:POOL_HALO + ts, :] = cur_ref[0]
    win_ref[POOL_HALO + ts:, :] = next_ref[0]
    kdim = POOL_SUB + 2 * POOL_HALO
    row = lax.broadcasted_iota(jnp.int32, (POOL_SUB, kdim), 0)
    col = lax.broadcasted_iota(jnp.int32, (POOL_SUB, kdim), 1)
    rel = col - row - POOL_HALO
    trow = lax.broadcasted_iota(jnp.int32, (POOL_SUB, 1), 0)
    for s in range(ts // POOL_SUB):
        base = t0 + s * POOL_SUB
        tok = base - POOL_HALO + col
        in_seq = (tok >= 0) & (tok < seq_len)
        t_abs = base + trow
        for g, w in enumerate(POOL_WINDOWS):
            half = w // 2
            band = jnp.where((rel >= -half) & (rel < w - half) & in_seq, 1.0, 0.0).astype(BF16)
            lo = jnp.maximum(t_abs - half, 0)
            hi = jnp.minimum(t_abs + (w - half), seq_len)
            cnt = (hi - lo).astype(F32)
            c0, c1 = g * POOL_GROUP_DIM, (g + 1) * POOL_GROUP_DIM
            uwin = win_ref[s * POOL_SUB:s * POOL_SUB + kdim, c0:c1]
            wsum = jnp.dot(band, uwin, preferred_element_type=F32)
            u = cur_ref[0, s * POOL_SUB:(s + 1) * POOL_SUB, c0:c1].astype(F32)
            delta = wsum / cnt - u
            y = jnp.dot(delta.astype(BF16), pw_ref[g], preferred_element_type=F32)
            o_ref[0, s * POOL_SUB:(s + 1) * POOL_SUB, c0:c1] = (y * ps_ref[:, c0:c1]).astype(BF16)


def _pool_mixer(proj3, pool_w_bf16, pool_scale):
    b, s, _ = proj3.shape
    ts = min(ROW_TILE, s)
    hb = ts // POOL_HALO
    n_halo = s // POOL_HALO
    return pl.pallas_call(
        functools.partial(_pool_kernel, s, ts),
        grid=(b, s // ts),
        in_specs=[
            pl.BlockSpec((1, ts, D_POOL), lambda bi, i: (bi, i, 0)),
            pl.BlockSpec((1, POOL_HALO, D_POOL), lambda bi, i: (bi, jnp.maximum(i * hb - 1, 0), 0)),
            pl.BlockSpec((1, POOL_HALO, D_POOL),
                         lambda bi, i: (bi, jnp.minimum((i + 1) * hb, n_halo - 1), 0)),
            pl.BlockSpec((len(POOL_WINDOWS), POOL_GROUP_DIM, POOL_GROUP_DIM), lambda bi, i: (0, 0, 0)),
            pl.BlockSpec((1, D_POOL), lambda bi, i: (0, 0)),
        ],
        out_specs=pl.BlockSpec((1, ts, D_POOL), lambda bi, i: (bi, i, 0)),
        out_shape=jax.ShapeDtypeStruct((b, s, D_POOL), BF16),
        scratch_shapes=[pltpu.VMEM((ts + 2 * POOL_HALO, D_POOL), BF16)],
        compiler_params=_params(("parallel", "parallel")),
        name="pool_mixer",
    )(proj3, proj3, proj3, pool_w_bf16, pool_scale.reshape(1, D_POOL))


def _na_bias_table(rpb):
    cols = np.arange(GRID_W)
    c0 = np.clip(cols - NA_KW // 2, 0, GRID_W - NA_KW)
    kc = np.arange(GRID_W)
    inside = (kc[None, :] >= c0[:, None]) & (kc[None, :] < c0[:, None] + NA_KW)
    rp = jnp.pad(rpb.astype(F32), ((0, 0), (0, 0), (GRID_W, GRID_W)))
    per_c = [rp[:, :, GRID_W + NA_KW - 1 - c:2 * GRID_W + NA_KW - 1 - c] for c in range(GRID_W)]
    by_dr = jnp.where(inside[None, None], jnp.stack(per_c, axis=2) * LOG2E, NEG_BIG)
    tab = jnp.stack([by_dr[:, d:d + NA_KH] for d in range(NA_KH)], axis=0)
    tab = jnp.transpose(tab, (0, 1, 3, 2, 4))
    return tab.reshape(NA_KH, NA_HEADS, GRID_W, NA_KH * GRID_W)


def _na_kernel(q_ref, *refs):
    k_refs = refs[:NA_KH]
    v_refs = refs[NA_KH:2 * NA_KH]
    bias_ref, o_ref, s_ref, p_ref = refs[2 * NA_KH:]
    gw = HEADS_PER_GROUP * NA_HEAD_DIM
    n_groups = NA_HEADS // HEADS_PER_GROUP
    lane_head = lax.broadcasted_iota(jnp.int32, (HEADS_PER_GROUP, 1, gw), 2) // NA_HEAD_DIM
    head_id = lax.broadcasted_iota(jnp.int32, (HEADS_PER_GROUP, 1, gw), 0)
    own = lane_head == head_id
    out_head = lax.broadcasted_iota(jnp.int32, (GRID_W, gw), 1) // NA_HEAD_DIM
    for bi in range(q_ref.shape[0]):
        for g in range(n_groups):
            slot = bi * n_groups + g
            lo, hi = g * gw, (g + 1) * gw
            qg = q_ref[bi, 0, :, lo:hi]
            zero = jnp.zeros_like(qg)
            qm = jnp.where(own, qg[None], zero[None]).reshape(HEADS_PER_GROUP * GRID_W, gw)
            kw = jnp.concatenate([r[bi, 0, :, lo:hi] for r in k_refs], axis=0)
            vw = jnp.concatenate([r[bi, 0, :, lo:hi] for r in v_refs], axis=0)
            s_ref[slot] = lax.dot_general(qm, kw, (((1,), (1,)), ((), ())), preferred_element_type=F32)
            inv_l = []
            for h in range(HEADS_PER_GROUP):
                rows = slice(h * GRID_W, (h + 1) * GRID_W)
                s = s_ref[slot, rows, :] + bias_ref[0, g * HEADS_PER_GROUP + h]
                p = jnp.exp2(s - jnp.max(s, axis=-1, keepdims=True))
                inv_l.append(1.0 / jnp.sum(p, axis=-1, keepdims=True))
                p_ref[slot, rows, :] = p.astype(BF16)
            pv = jnp.dot(p_ref[slot], vw, preferred_element_type=F32)
            out = pv[:GRID_W] * inv_l[0]
            for h in range(1, HEADS_PER_GROUP):
                out = jnp.where(out_head == h, pv[h * GRID_W:(h + 1) * GRID_W] * inv_l[h], out)
            o_ref[bi, 0, :, lo:hi] = out.astype(BF16)


def _neighbourhood_attention(proj4, bias_tab):
    b, rows, w, _ = proj4.shape
    assert w == GRID_W and rows >= NA_KH
    n_chains = b * (NA_HEADS // HEADS_PER_GROUP)
    n_stack = HEADS_PER_GROUP * GRID_W
    n_keys = NA_KH * GRID_W

    def r0(r):
        return jnp.clip(r - NA_KH // 2, 0, rows - NA_KH)

    def kv_spec(i, col_block):
        return pl.BlockSpec((b, 1, GRID_W, D_ATTN), lambda r: (0, r0(r) + i, 0, col_block))

    in_specs = [pl.BlockSpec((b, 1, GRID_W, D_ATTN), lambda r: (0, r, 0, 1))]
    in_specs += [kv_spec(i, 2) for i in range(NA_KH)]
    in_specs += [kv_spec(i, 3) for i in range(NA_KH)]
    in_specs += [pl.BlockSpec((1, NA_HEADS, GRID_W, NA_KH * GRID_W),
                              lambda r: (r0(r) - r + NA_KH - 1, 0, 0, 0))]
    return pl.pallas_call(
        _na_kernel,
        grid=(rows,),
        in_specs=in_specs,
        out_specs=pl.BlockSpec((b, 1, GRID_W, D_ATTN), lambda r: (0, r, 0, 0)),
        out_shape=jax.ShapeDtypeStruct((b, rows, GRID_W, D_ATTN), BF16),
        scratch_shapes=[pltpu.VMEM((n_chains, n_stack, n_keys), F32),
                        pltpu.VMEM((n_chains, n_stack, n_keys), BF16)],
        compiler_params=_params(("arbitrary",)),
        name="neighbourhood_attention",
    )(*([proj4] * (1 + 2 * NA_KH)), bias_tab)


def _out_proj_kernel(with_router, yp_ref, ya_ref, w_ref, h_ref, gpost_ref, gpre_ref, *refs):
    mix = jnp.dot(yp_ref[...], w_ref[:D_POOL, :], preferred_element_type=F32)
    mix = mix + jnp.dot(ya_ref[...], w_ref[D_POOL:, :], preferred_element_type=F32)
    h1 = h_ref[...] + _rms(mix, gpost_ref[...])
    hn = _rms(h1, gpre_ref[...])
    if with_router:
        wr_ref, h1_ref, hn_ref, logit_ref = refs
        hn_ref[...] = hn
        logit_ref[...] = jnp.dot(hn.astype(BF16), wr_ref[...], preferred_element_type=F32)
    else:
        h1_ref, hn_ref = refs
        hn_ref[...] = hn.astype(BF16)
    h1_ref[...] = h1


def _out_proj(y_pool, y_na, w_out_bf16, h, g_post, g_pre, router_bf16=None):
    t = h.shape[0]
    tm = min(ROW_TILE, t)
    with_router = router_bf16 is not None
    row = lambda width: pl.BlockSpec((tm, width), lambda i: (i, 0))
    full = lambda a, b: pl.BlockSpec((a, b), lambda i: (0, 0))
    in_specs = [row(D_POOL), row(D_ATTN), full(D_MODEL, D_MODEL), row(D_MODEL),
                full(1, D_MODEL), full(1, D_MODEL)]
    args = [y_pool, y_na, w_out_bf16, h, g_post.reshape(1, D_MODEL), g_pre.reshape(1, D_MODEL)]
    out_specs = [row(D_MODEL), row(D_MODEL)]
    out_shape = [jax.ShapeDtypeStruct((t, D_MODEL), F32),
                 jax.ShapeDtypeStruct((t, D_MODEL), F32 if with_router else BF16)]
    if with_router:
        in_specs.append(full(D_MODEL, LANES))
        args.append(router_bf16)
        out_specs.append(row(LANES))
        out_shape.append(jax.ShapeDtypeStruct((t, LANES), F32))
    return pl.pallas_call(
        functools.partial(_out_proj_kernel, with_router),
        grid=(t // tm,),
        in_specs=in_specs,
        out_specs=out_specs,
        out_shape=out_shape,
        compiler_params=_params(("parallel",)),
        name="out_proj_router" if with_router else "out_proj",
    )(*args)


def _swiglu_kernel(fuse_norm, be_ref, nused_ref, x_ref, wg_ref, wu_ref, wd_ref, *refs):
    if fuse_norm:
        h_ref, g_ref, o_ref, acc_ref = refs
    else:
        o_ref, acc_ref = refs
    j = pl.program_id(0)
    f = pl.program_id(1)
    live = j < nused_ref[0]

    @pl.when(live)
    def _():
        x = x_ref[...].astype(BF16)
        a = jnp.dot(x, wg_ref[0], preferred_element_type=F32)
        b = jnp.dot(x, wu_ref[0], preferred_element_type=F32)
        hmid = (a * jax.nn.sigmoid(a) * b).astype(BF16)
        part = jnp.dot(hmid, wd_ref[0], preferred_element_type=F32)

        @pl.when(f == 0)
        def _():
            acc_ref[...] = part

        @pl.when(f > 0)
        def _():
            acc_ref[...] += part

    @pl.when(f == pl.num_programs(1) - 1)
    def _():
        @pl.when(live)
        def _():
            if fuse_norm:
                o_ref[...] = h_ref[...] + _rms(acc_ref[...], g_ref[...])
            else:
                o_ref[...] = acc_ref[...]

        @pl.when(jnp.logical_not(live))
        def _():
            o_ref[...] = jnp.zeros_like(o_ref)


def _swiglu(x, wg, wu, wd, block_expert, n_used, resid=None, g_post=None):
    rows = x.shape[0]
    dff = wg.shape[-1]
    bm = min(FFN_ROWS, rows)
    tf = max(c for c in range(MXU_TILE, min(FFN_COLS_MAX, dff) + 1, MXU_TILE) if dff % c == 0)
    assert rows % bm == 0 and dff % tf == 0
    nf = dff // tf
    fuse_norm = resid is not None

    def f_eff(j, f, nu):
        return jnp.where(j < nu[0], f, nf - 1)

    in_specs = [
        pl.BlockSpec((bm, D_MODEL), lambda j, f, be, nu: (j, 0)),
        pl.BlockSpec((1, D_MODEL, tf), lambda j, f, be, nu: (be[j], 0, f_eff(j, f, nu))),
        pl.BlockSpec((1, D_MODEL, tf), lambda j, f, be, nu: (be[j], 0, f_eff(j, f, nu))),
        pl.BlockSpec((1, tf, D_MODEL), lambda j, f, be, nu: (be[j], f_eff(j, f, nu), 0)),
    ]
    args = [x, wg, wu, wd]
    if fuse_norm:
        in_specs += [pl.BlockSpec((bm, D_MODEL), lambda j, f, be, nu: (j, 0)),
                     pl.BlockSpec((1, D_MODEL), lambda j, f, be, nu: (0, 0))]
        args += [resid, g_post.reshape(1, D_MODEL)]
    return pl.pallas_call(
        functools.partial(_swiglu_kernel, fuse_norm),
        grid_spec=pltpu.PrefetchScalarGridSpec(
            num_scalar_prefetch=2,
            grid=(rows // bm, nf),
            in_specs=in_specs,
            out_specs=pl.BlockSpec((bm, D_MODEL), lambda j, f, be, nu: (j, 0)),
            scratch_shapes=[pltpu.VMEM((bm, D_MODEL), F32)],
        ),
        out_shape=jax.ShapeDtypeStruct((rows, D_MODEL), F32),
        compiler_params=_params(("parallel", "arbitrary")),
        name="swiglu_dense" if fuse_norm else "swiglu_experts",
    )(block_expert, n_used, *args)


def _dispatch_kernel(dest_ref, hn_ref, xs_in_ref, xs_ref, sem):
    del xs_in_ref
    n_tok = hn_ref.shape[0]

    def copy(i, k):
        return pltpu.make_async_copy(hn_ref.at[pl.ds(i, 1)],
                                     xs_ref.at[pl.ds(dest_ref[0, 0, TOP_K * i + k], 1)], sem)

    def start(i, c):
        for k in range(TOP_K):
            copy(i, k).start()
        return c

    def wait(i, c):
        for k in range(TOP_K):
            copy(i, k).wait()
        return c

    lax.fori_loop(0, n_tok, start, 0)
    lax.fori_loop(0, n_tok, wait, 0)


def _dispatch(hn, dest, n_rows):
    t = hn.shape[0]
    tb = min(MOE_TOK_TILE, t)
    xs0 = jnp.zeros((n_rows, D_MODEL), F32)
    return pl.pallas_call(
        _dispatch_kernel,
        grid=(t // tb,),
        in_specs=[
            pl.BlockSpec((1, 1, TOP_K * tb), lambda i: (i, 0, 0), memory_space=pltpu.SMEM),
            pl.BlockSpec((tb, D_MODEL), lambda i: (i, 0)),
            pl.BlockSpec(memory_space=pl.ANY),
        ],
        out_specs=pl.BlockSpec(memory_space=pl.ANY),
        out_shape=jax.ShapeDtypeStruct((n_rows, D_MODEL), F32),
        scratch_shapes=[pltpu.SemaphoreType.DMA(())],
        input_output_aliases={2: 0},
        compiler_params=_params(("arbitrary",)),
        name="moe_dispatch",
    )(dest.reshape(t // tb, 1, TOP_K * tb), hn, xs0)


def _combine_kernel(dest_ref, ys_ref, gate_ref, h_ref, g_ref, o_ref, buf_ref, sem):
    n_tok = h_ref.shape[0]

    def copy(i, k):
        return pltpu.make_async_copy(ys_ref.at[pl.ds(dest_ref[0, 0, TOP_K * i + k], 1)],
                                     buf_ref.at[k, pl.ds(i, 1)], sem)

    def start(i, c):
        for k in range(TOP_K):
            copy(i, k).start()
        return c

    def wait(i, c):
        for k in range(TOP_K):
            copy(i, k).wait()
        return c

    lax.fori_loop(0, n_tok, start, 0)
    lax.fori_loop(0, n_tok, wait, 0)
    f = buf_ref[0] * gate_ref[:, 0:1]
    for k in range(1, TOP_K):
        f = f + buf_ref[k] * gate_ref[:, k:k + 1]
    o_ref[...] = h_ref[...] + _rms(f, g_ref[...])


def _combine(ys, dest, gates, h, g_post):
    t = h.shape[0]
    tb = min(MOE_TOK_TILE, t)
    return pl.pallas_call(
        _combine_kernel,
        grid=(t // tb,),
        in_specs=[
            pl.BlockSpec((1, 1, TOP_K * tb), lambda i: (i, 0, 0), memory_space=pltpu.SMEM),
            pl.BlockSpec(memory_space=pl.ANY),
            pl.BlockSpec((tb, TOP_K), lambda i: (i, 0)),
            pl.BlockSpec((tb, D_MODEL), lambda i: (i, 0)),
            pl.BlockSpec((1, D_MODEL), lambda i: (0, 0)),
        ],
        out_specs=pl.BlockSpec((tb, D_MODEL), lambda i: (i, 0)),
        out_shape=jax.ShapeDtypeStruct((t, D_MODEL), F32),
        scratch_shapes=[pltpu.VMEM((TOP_K, tb, D_MODEL), F32), pltpu.SemaphoreType.DMA(())],
        compiler_params=_params(("arbitrary",)),
        name="moe_combine",
    )(dest.reshape(t // tb, 1, TOP_K * tb), ys, gates, h, g_post.reshape(1, D_MODEL))


def _route(logits, bm):
    t = logits.shape[0]
    lane = lax.broadcasted_iota(jnp.int32, logits.shape, 1)
    i1 = jnp.argmax(logits, axis=-1).astype(jnp.int32)
    l1 = jnp.max(logits, axis=-1)
    rest = jnp.where(lane == i1[:, None], -jnp.inf, logits)
    i2 = jnp.argmax(rest, axis=-1).astype(jnp.int32)
    l2 = jnp.max(rest, axis=-1)
    gates = jax.nn.softmax(jnp.stack([l1, l2], axis=-1), axis=-1)
    e_flat = jnp.stack([i1, i2], axis=-1).reshape(t * TOP_K)
    onehot = (e_flat[:, None] == jnp.arange(N_EXPERTS, dtype=jnp.int32)[None, :]).astype(jnp.int32)
    csum = jnp.cumsum(onehot, axis=0)
    rank = jnp.sum((csum - onehot) * onehot, axis=-1)
    counts = csum[-1]
    padded = (counts + bm - 1) // bm * bm
    pad_end = jnp.cumsum(padded)
    pad_start = pad_end - padded
    dest = (pad_start[e_flat] + rank).astype(jnp.int32)
    n_blocks = (t * TOP_K) // bm + N_EXPERTS
    n_used = (pad_end[-1] // bm).astype(jnp.int32)
    blk = jnp.minimum(jnp.arange(n_blocks, dtype=jnp.int32), n_used - 1) * bm
    block_expert = jnp.minimum(jnp.searchsorted(pad_end, blk, side="right"), N_EXPERTS - 1)
    return dest, gates, block_expert.astype(jnp.int32), n_used.reshape(1), n_blocks * bm


def kernel(x, mix_norm_pre, mix_norm_post, ffn_norm_pre, ffn_norm_post, w_in, pool_w, pool_scale, na_rpb,
           w_out, dense_w_gate, dense_w_up, dense_w_down, moe_router, moe_w_gate, moe_w_up, moe_w_down):
    b, s, d = x.shape
    assert d == D_MODEL and s % GRID_W == 0
    t = b * s
    rows = s // GRID_W
    depth = w_in.shape[0]
    h = x.reshape(t, d)
    for layer in range(depth):
        j = layer // 2
        is_moe = layer % 2 == 1
        proj = _norm_proj(h, mix_norm_pre[layer], w_in[layer].astype(BF16))
        y_pool = _pool_mixer(proj.reshape(b, s, D_IN), pool_w[layer].astype(BF16), pool_scale[layer])
        y_na = _neighbourhood_attention(proj.reshape(b, rows, GRID_W, D_IN), _na_bias_table(na_rpb[layer]))
        router = None
        if is_moe:
            router = jnp.pad(moe_router[j], ((0, 0), (0, LANES - N_EXPERTS))).astype(BF16)
        outs = _out_proj(y_pool.reshape(t, D_POOL), y_na.reshape(t, D_ATTN), w_out[layer].astype(BF16), h,
                         mix_norm_post[layer], ffn_norm_pre[layer], router)
        if is_moe:
            h1, hn, logits = outs
            bm = min(FFN_ROWS, t)
            dest, gates, block_expert, n_used, n_rows = _route(logits[:, :N_EXPERTS], bm)
            xs = _dispatch(hn, dest, n_rows)
            ys = _swiglu(xs, moe_w_gate[j].astype(BF16), moe_w_up[j].astype(BF16),
                         moe_w_down[j].astype(BF16), block_expert, n_used)
            h = _combine(ys, dest, gates, h1, ffn_norm_post[layer])
        else:
            h1, hn = outs
            n_blk = t // min(FFN_ROWS, t)
            h = _swiglu(hn, dense_w_gate[j][None].astype(BF16), dense_w_up[j][None].astype(BF16),
                        dense_w_down[j][None].astype(BF16), jnp.zeros((n_blk,), jnp.int32),
                        jnp.full((1,), n_blk, jnp.int32), resid=h1, g_post=ffn_norm_post[layer])
    return h.reshape(b, s, d)
```

```python
import functools

import jax
import jax.numpy as jnp
import numpy as np
from jax import lax
from jax.experimental import pallas as pl
from jax.experimental.pallas import tpu as pltpu
from jax.experimental.pallas import tpu_sc as plsc

F32 = jnp.float32
BF16 = jnp.bfloat16

D_MODEL = 1024
D_POOL = 512
POOL_WINDOWS = (2, 4, 8, 16)
POOL_GROUP_DIM = 128
D_ATTN = 512
NA_HEAD_DIM = 32
NA_HEADS = 16
D_IN = D_POOL + 3 * D_ATTN
GRID_W = 64
NA_KH = 8
NA_KW = 16
N_EXPERTS = 8
TOP_K = 2
RMS_EPS = 1e-6
NEG_BIG = -1e30
LOG2E = 1.4426950408889634

LANES = 128
HEADS_PER_GROUP = 256 // NA_HEAD_DIM
VMEM_LIMIT = 52 * 1024 * 1024

ROW_TILE = 512
POOL_SUB = 128
POOL_HALO = 64
FFN_ROWS = 512
FFN_COLS_MAX = 2048
MXU_TILE = 256
SC_WINDOW = 128
ROW_SPLIT = 2


def _rms(x, g):
    ms = jnp.mean(x * x, axis=-1, keepdims=True)
    return x * lax.rsqrt(ms + RMS_EPS) * g


def _params(sem):
    return pltpu.CompilerParams(dimension_semantics=sem, vmem_limit_bytes=VMEM_LIMIT)


def _norm_proj_kernel(x_ref, g_ref, w_ref, o_ref):
    hn = _rms(x_ref[...], g_ref[...])
    p = jnp.dot(hn.astype(BF16), w_ref[...], preferred_element_type=F32)
    q_lo, q_hi = D_POOL, D_POOL + D_ATTN
    o_ref[:, :q_lo] = p[:, :q_lo].astype(BF16)
    o_ref[:, q_lo:q_hi] = (p[:, q_lo:q_hi] * (NA_HEAD_DIM ** -0.5 * LOG2E)).astype(BF16)
    o_ref[:, q_hi:] = p[:, q_hi:].astype(BF16)


def _norm_proj(h, g, w_bf16):
    t = h.shape[0]
    tm = min(ROW_TILE, t)
    return pl.pallas_call(
        _norm_proj_kernel,
        grid=(t // tm,),
        in_specs=[
            pl.BlockSpec((tm, D_MODEL), lambda i: (i, 0)),
            pl.BlockSpec((1, D_MODEL), lambda i: (0, 0)),
            pl.BlockSpec((D_MODEL, D_IN), lambda i: (0, 0)),
        ],
        out_specs=pl.BlockSpec((tm, D_IN), lambda i: (i, 0)),
        out_shape=jax.ShapeDtypeStruct((t, D_IN), BF16),
        compiler_params=_params(("parallel",)),
        name="norm_proj",
    )(h, g.reshape(1, D_MODEL), w_bf16)


def _pool_bands():
    kdim = POOL_SUB + 2 * POOL_HALO
    rel = np.arange(kdim)[None, :] - np.arange(POOL_SUB)[:, None] - POOL_HALO
    return np.stack([((rel >= -(w // 2)) & (rel < w - w // 2)) for w in POOL_WINDOWS]).astype(np.float32)


def _pool_kernel(seq_len, ts, cur_ref, prev_ref, next_ref, band_ref, pw_ref, ps_ref, o_ref, win_ref):
    i = pl.program_id(1)
    t0 = i * ts
    halo_zero = jnp.zeros((POOL_HALO, D_POOL), BF16)
    win_ref[:POOL_HALO, :] = jnp.where(i == 0, halo_zero, prev_ref[0])
    win_ref[POOL_HALO:POOL_HALO + ts, :] = cur_ref[0]
    win_ref[POOL_HALO + ts:, :] = jnp.where(i == pl.num_programs(1) - 1, halo_zero, next_ref[0])
    kdim = POOL_SUB + 2 * POOL_HALO
    t_abs = t0 + lax.broadcasted_iota(jnp.int32, (ts, 1), 0)
    for g, w in enumerate(POOL_WINDOWS):
        half = w // 2
        c0, c1 = g * POOL_GROUP_DIM, (g + 1) * POOL_GROUP_DIM
        cnt = (jnp.minimum(t_abs + (w - half), seq_len) - jnp.maximum(t_abs - half, 0)).astype(F32)
        wsum = jnp.concatenate(
            [jnp.dot(band_ref[g], win_ref[s * POOL_SUB:s * POOL_SUB + kdim, c0:c1], preferred_element_type=F32)
             for s in range(ts // POOL_SUB)], axis=0)
        delta = wsum / cnt - cur_ref[0, :, c0:c1].astype(F32)
        y = jnp.dot(delta.astype(BF16), pw_ref[g], preferred_element_type=F32)
        o_ref[0, :, c0:c1] = (y * ps_ref[:, c0:c1]).astype(BF16)


def _pool_mixer(proj3, pool_w_bf16, pool_scale):
    b, s, _ = proj3.shape
    ts = min(ROW_TILE, s)
    hb = ts // POOL_HALO
    n_halo = s // POOL_HALO
    bands = jnp.asarray(_pool_bands(), BF16)
    return pl.pallas_call(
        functools.partial(_pool_kernel, s, ts),
        grid=(b, s // ts),
        in_specs=[
            pl.BlockSpec((1, ts, D_POOL), lambda bi, i: (bi, i, 0)),
            pl.BlockSpec((1, POOL_HALO, D_POOL), lambda bi, i: (bi, jnp.maximum(i * hb - 1, 0), 0)),
            pl.BlockSpec((1, POOL_HALO, D_POOL),
                         lambda bi, i: (bi, jnp.minimum((i + 1) * hb, n_halo - 1), 0)),
            pl.BlockSpec(bands.shape, lambda bi, i: (0, 0, 0)),
            pl.BlockSpec((len(POOL_WINDOWS), POOL_GROUP_DIM, POOL_GROUP_DIM), lambda bi, i: (0, 0, 0)),
            pl.BlockSpec((1, D_POOL), lambda bi, i: (0, 0)),
        ],
        out_specs=pl.BlockSpec((1, ts, D_POOL), lambda bi, i: (bi, i, 0)),
        out_shape=jax.ShapeDtypeStruct((b, s, D_POOL), BF16),
        scratch_shapes=[pltpu.VMEM((ts + 2 * POOL_HALO, D_POOL), BF16)],
        compiler_params=_params(("parallel", "parallel")),
        name="pool_mixer",
    )(proj3, proj3, proj3, bands, pool_w_bf16, pool_scale.reshape(1, D_POOL))


def _na_bias_table(rpb):
    cols = np.arange(GRID_W)
    c0 = np.clip(cols - NA_KW // 2, 0, GRID_W - NA_KW)
    kc = np.arange(GRID_W)
    inside = (kc[None, :] >= c0[:, None]) & (kc[None, :] < c0[:, None] + NA_KW)
    rp = jnp.pad(rpb.astype(F32), ((0, 0), (0, 0), (GRID_W, GRID_W)))
    per_c = [rp[:, :, GRID_W + NA_KW - 1 - c:2 * GRID_W + NA_KW - 1 - c] for c in range(GRID_W)]
    by_dr = jnp.where(inside[None, None], jnp.stack(per_c, axis=2) * LOG2E, NEG_BIG)
    tab = jnp.stack([by_dr[:, d:d + NA_KH] for d in range(NA_KH)], axis=0)
    tab = jnp.transpose(tab, (0, 1, 3, 2, 4))
    return tab.reshape(NA_KH, NA_HEADS, GRID_W, NA_KH * GRID_W)


def _na_kernel(q_ref, *refs):
    k_refs = refs[:NA_KH]
    v_refs = refs[NA_KH:2 * NA_KH]
    bias_ref, o_ref, s_ref, p_ref = refs[2 * NA_KH:]
    gw = HEADS_PER_GROUP * NA_HEAD_DIM
    n_groups = NA_HEADS // HEADS_PER_GROUP
    lane_head = lax.broadcasted_iota(jnp.int32, (HEADS_PER_GROUP, 1, gw), 2) // NA_HEAD_DIM
    head_id = lax.broadcasted_iota(jnp.int32, (HEADS_PER_GROUP, 1, gw), 0)
    own = lane_head == head_id
    out_head = lax.broadcasted_iota(jnp.int32, (GRID_W, gw), 1) // NA_HEAD_DIM
    for bi in range(q_ref.shape[0]):
        for g in range(n_groups):
            slot = bi * n_groups + g
            lo, hi = g * gw, (g + 1) * gw
            qg = q_ref[bi, 0, :, lo:hi]
            zero = jnp.zeros_like(qg)
            qm = jnp.where(own, qg[None], zero[None]).reshape(HEADS_PER_GROUP * GRID_W, gw)
            kw = jnp.concatenate([r[bi, 0, :, lo:hi] for r in k_refs], axis=0)
            vw = jnp.concatenate([r[bi, 0, :, lo:hi] for r in v_refs], axis=0)
            s_ref[slot] = lax.dot_general(qm, kw, (((1,), (1,)), ((), ())), preferred_element_type=F32)
            inv_l = []
            for h in range(HEADS_PER_GROUP):
                rows = slice(h * GRID_W, (h + 1) * GRID_W)
                s = s_ref[slot, rows, :] + bias_ref[0, g * HEADS_PER_GROUP + h]
                p = jnp.exp2(s - jnp.max(s, axis=-1, keepdims=True))
                inv_l.append(1.0 / jnp.sum(p, axis=-1, keepdims=True))
                p_ref[slot, rows, :] = p.astype(BF16)
            pv = jnp.dot(p_ref[slot], vw, preferred_element_type=F32)
            out = pv[:GRID_W] * inv_l[0]
            for h in range(1, HEADS_PER_GROUP):
                out = jnp.where(out_head == h, pv[h * GRID_W:(h + 1) * GRID_W] * inv_l[h], out)
            o_ref[bi, 0, :, lo:hi] = out.astype(BF16)


def _neighbourhood_attention(proj4, bias_tab):
    b, rows, w, _ = proj4.shape
    assert w == GRID_W and rows >= NA_KH
    n_chains = b * (NA_HEADS // HEADS_PER_GROUP)
    n_stack = HEADS_PER_GROUP * GRID_W
    n_keys = NA_KH * GRID_W

    def r0(r):
        return jnp.clip(r - NA_KH // 2, 0, rows - NA_KH)

    def kv_spec(i, col_block):
        return pl.BlockSpec((b, 1, GRID_W, D_ATTN), lambda r: (0, r0(r) + i, 0, col_block))

    in_specs = [pl.BlockSpec((b, 1, GRID_W, D_ATTN), lambda r: (0, r, 0, 1))]
    in_specs += [kv_spec(i, 2) for i in range(NA_KH)]
    in_specs += [kv_spec(i, 3) for i in range(NA_KH)]
    in_specs += [pl.BlockSpec((1, NA_HEADS, GRID_W, NA_KH * GRID_W),
                              lambda r: (r0(r) - r + NA_KH - 1, 0, 0, 0))]
    return pl.pallas_call(
        _na_kernel,
        grid=(rows,),
        in_specs=in_specs,
        out_specs=pl.BlockSpec((b, 1, GRID_W, D_ATTN), lambda r: (0, r, 0, 0)),
        out_shape=jax.ShapeDtypeStruct((b, rows, GRID_W, D_ATTN), BF16),
        scratch_shapes=[pltpu.VMEM((n_chains, n_stack, n_keys), F32),
                        pltpu.VMEM((n_chains, n_stack, n_keys), BF16)],
        compiler_params=_params(("arbitrary",)),
        name="neighbourhood_attention",
    )(*([proj4] * (1 + 2 * NA_KH)), bias_tab)


def _pack_bf16_pairs(x):
    n = x.shape[1] // 2
    hi = lax.bitcast_convert_type(x[:, :n].astype(BF16).astype(F32), jnp.uint32)
    lo = lax.bitcast_convert_type(x[:, n:].astype(BF16).astype(F32), jnp.uint32)
    return hi | (lo >> 16)


def _unpack_bf16_pairs(w):
    hi = lax.bitcast_convert_type(w & jnp.uint32(0xFFFF0000), F32)
    lo = lax.bitcast_convert_type(w << 16, F32)
    return jnp.concatenate([hi, lo], axis=1)


SLAB = D_MODEL // 2 // ROW_SPLIT


def _store_slabs(slab_refs, words):
    for p, ref in enumerate(slab_refs):
        ref[...] = words[:, p * SLAB:(p + 1) * SLAB]


def _load_slabs(slab_refs, lead=()):
    return jnp.concatenate([ref[lead + (slice(None), slice(None))] for ref in slab_refs], axis=1)


def _out_proj_kernel(with_router, yp_ref, ya_ref, w_ref, h_ref, gpost_ref, gpre_ref, *refs):
    mix = jnp.dot(yp_ref[...], w_ref[:D_POOL, :], preferred_element_type=F32)
    mix = mix + jnp.dot(ya_ref[...], w_ref[D_POOL:, :], preferred_element_type=F32)
    h1 = h_ref[...] + _rms(mix, gpost_ref[...])
    hn = _rms(h1, gpre_ref[...])
    if with_router:
        wr_ref, h1_ref, logit_ref = refs[:3]
        _store_slabs(refs[3:], _pack_bf16_pairs(hn))
        logit_ref[...] = jnp.dot(hn.astype(BF16), wr_ref[...], preferred_element_type=F32)
    else:
        h1_ref, hn_ref = refs
        hn_ref[...] = hn.astype(BF16)
    h1_ref[...] = h1


def _out_proj(y_pool, y_na, w_out_bf16, h, g_post, g_pre, router_bf16=None):
    t = h.shape[0]
    tm = min(ROW_TILE, t)
    with_router = router_bf16 is not None
    row = lambda width: pl.BlockSpec((tm, width), lambda i: (i, 0))
    full = lambda a, b: pl.BlockSpec((a, b), lambda i: (0, 0))
    in_specs = [row(D_POOL), row(D_ATTN), full(D_MODEL, D_MODEL), row(D_MODEL),
                full(1, D_MODEL), full(1, D_MODEL)]
    args = [y_pool, y_na, w_out_bf16, h, g_post.reshape(1, D_MODEL), g_pre.reshape(1, D_MODEL)]
    if with_router:
        in_specs.append(full(D_MODEL, LANES))
        args.append(router_bf16)
        out_specs = [row(D_MODEL), row(LANES)] + [row(SLAB)] * ROW_SPLIT
        out_shape = [jax.ShapeDtypeStruct((t, D_MODEL), F32), jax.ShapeDtypeStruct((t, LANES), F32)]
        out_shape += [jax.ShapeDtypeStruct((t, SLAB), jnp.uint32)] * ROW_SPLIT
    else:
        out_specs = [row(D_MODEL), row(D_MODEL)]
        out_shape = [jax.ShapeDtypeStruct((t, D_MODEL), F32), jax.ShapeDtypeStruct((t, D_MODEL), BF16)]
    return pl.pallas_call(
        functools.partial(_out_proj_kernel, with_router),
        grid=(t // tm,),
        in_specs=in_specs,
        out_specs=out_specs,
        out_shape=out_shape,
        compiler_params=_params(("parallel",)),
        name="out_proj_router" if with_router else "out_proj",
    )(*args)


def _swiglu_kernel(dense, be_ref, nused_ref, nvalid_ref, *refs):
    if dense:
        x_ref, wg_ref, wu_ref, wd_ref, h_ref, g_ref, o_ref, acc_ref = refs
    else:
        x_refs, (wg_ref, wu_ref, wd_ref) = refs[:ROW_SPLIT], refs[ROW_SPLIT:ROW_SPLIT + 3]
        o_refs, acc_ref = refs[ROW_SPLIT + 3:-1], refs[-1]
    j = pl.program_id(0)
    f = pl.program_id(1)
    live = j < nused_ref[0]

    @pl.when(live)
    def _():
        if dense:
            x = x_ref[...]
        else:
            row = lax.broadcasted_iota(jnp.int32, (acc_ref.shape[0], 1), 0)
            words = jnp.where(row < nvalid_ref[j], _load_slabs(x_refs), jnp.uint32(0))
            x = _unpack_bf16_pairs(words).astype(BF16)
        a = jnp.dot(x, wg_ref[0], preferred_element_type=F32)
        b = jnp.dot(x, wu_ref[0], preferred_element_type=F32)
        hmid = (a * jax.nn.sigmoid(a) * b).astype(BF16)
        part = jnp.dot(hmid, wd_ref[0], preferred_element_type=F32)

        @pl.when(f == 0)
        def _():
            acc_ref[...] = part

        @pl.when(f > 0)
        def _():
            acc_ref[...] += part

    @pl.when(f == pl.num_programs(1) - 1)
    def _():
        @pl.when(live)
        def _():
            if dense:
                o_ref[...] = h_ref[...] + _rms(acc_ref[...], g_ref[...])
            else:
                _store_slabs(o_refs, _pack_bf16_pairs(acc_ref[...]))

        @pl.when(jnp.logical_not(live))
        def _():
            for ref in ([o_ref] if dense else o_refs):
                ref[...] = jnp.zeros_like(ref)


def _swiglu(x, wg, wu, wd, block_expert, n_used, n_valid, resid=None, g_post=None):
    dense = resid is not None
    xs = [x] if dense else list(x)
    rows = xs[0].shape[0]
    dff = wg.shape[-1]
    bm = min(FFN_ROWS, rows)
    tf = max(c for c in range(MXU_TILE, min(FFN_COLS_MAX, dff) + 1, MXU_TILE) if dff % c == 0)
    assert rows % bm == 0 and dff % tf == 0
    nf = dff // tf

    def f_eff(j, f, nu):
        return jnp.where(j < nu[0], f, nf - 1)

    row_spec = lambda width: pl.BlockSpec((bm, width), lambda j, f, be, nu, nv: (j, 0))
    in_specs = [row_spec(a.shape[1]) for a in xs] + [
        pl.BlockSpec((1, D_MODEL, tf), lambda j, f, be, nu, nv: (be[j], 0, f_eff(j, f, nu))),
        pl.BlockSpec((1, D_MODEL, tf), lambda j, f, be, nu, nv: (be[j], 0, f_eff(j, f, nu))),
        pl.BlockSpec((1, tf, D_MODEL), lambda j, f, be, nu, nv: (be[j], f_eff(j, f, nu), 0)),
    ]
    args = xs + [wg, wu, wd]
    if dense:
        in_specs += [row_spec(D_MODEL), pl.BlockSpec((1, D_MODEL), lambda j, f, be, nu, nv: (0, 0))]
        args += [resid, g_post.reshape(1, D_MODEL)]
        out_shape, out_specs = jax.ShapeDtypeStruct((rows, D_MODEL), F32), row_spec(D_MODEL)
    else:
        out_shape = [jax.ShapeDtypeStruct((rows, SLAB), jnp.uint32)] * ROW_SPLIT
        out_specs = [row_spec(SLAB)] * ROW_SPLIT
    return pl.pallas_call(
        functools.partial(_swiglu_kernel, dense),
        grid_spec=pltpu.PrefetchScalarGridSpec(
            num_scalar_prefetch=3,
            grid=(rows // bm, nf),
            in_specs=in_specs,
            out_specs=out_specs,
            scratch_shapes=[pltpu.VMEM((bm, D_MODEL), F32)],
        ),
        out_shape=out_shape,
        compiler_params=_params(("parallel", "arbitrary")),
        name="swiglu_dense" if dense else "swiglu_experts",
    )(block_expert, n_used, n_valid, *args)


def _sc_mesh():
    return plsc.VectorSubcoreMesh(core_axis_name="core", subcore_axis_name="subcore")


def _sc_dispatch(slabs, dest, n_rows):
    t, width = slabs[0].shape
    idx = [dest[:, k].reshape(1, t) for k in range(TOP_K)]
    n_slabs = len(slabs)

    @functools.partial(pl.kernel, mesh=_sc_mesh(), scratch_types=[], name="moe_dispatch_sc",
                       out_type=[jax.ShapeDtypeStruct((n_rows, width), slabs[0].dtype)] * n_slabs)
    def scatter_rows(*refs):
        x_hbm, idx_hbm, o_hbm = refs[:n_slabs], refs[n_slabs:n_slabs + TOP_K], refs[n_slabs + TOP_K:]
        for p in range(n_slabs):
            def body(x_vmem, *idx_vmem, out=o_hbm[p]):
                for i_vmem in idx_vmem:
                    pltpu.sync_copy(x_vmem, out.at[i_vmem.at[0]])

            pltpu.emit_pipeline(
                body,
                grid=(t // SC_WINDOW,),
                in_specs=[pl.BlockSpec((SC_WINDOW, width), lambda i: (i, 0))]
                + [pl.BlockSpec((1, SC_WINDOW), lambda i: (0, i))] * TOP_K,
                out_specs=[],
                core_axis_name=("core", "subcore"),
                dimension_semantics=(pltpu.PARALLEL,),
            )(x_hbm[p], *idx_hbm)

    return scatter_rows(*slabs, *idx)


def _sc_gather(slabs, idx):
    n = idx.shape[0]
    width = slabs[0].shape[1]
    n_slabs = len(slabs)

    @functools.partial(pl.kernel, mesh=_sc_mesh(), scratch_types=[], name="moe_gather_sc",
                       out_type=[jax.ShapeDtypeStruct((n, width), slabs[0].dtype)] * n_slabs)
    def gather_rows(*refs):
        x_hbm, i_hbm, o_hbm = refs[:n_slabs], refs[n_slabs], refs[n_slabs + 1:]
        for p in range(n_slabs):
            def body(i_vmem, o_vmem, src=x_hbm[p]):
                pltpu.sync_copy(src.at[i_vmem.at[0]], o_vmem)

            pltpu.emit_pipeline(
                body,
                grid=(n // SC_WINDOW,),
                in_specs=[pl.BlockSpec((1, SC_WINDOW), lambda i: (0, i))],
                out_specs=[pl.BlockSpec((SC_WINDOW, width), lambda i: (i, 0))],
                core_axis_name=("core", "subcore"),
                dimension_semantics=(pltpu.PARALLEL,),
            )(i_hbm, o_hbm[p])

    return gather_rows(*slabs, idx.reshape(1, n))


def _combine_kernel(*refs):
    y_refs, (gate_ref, h_ref, g_ref, o_ref) = refs[:ROW_SPLIT], refs[ROW_SPLIT:]
    f = _unpack_bf16_pairs(_load_slabs(y_refs, (0,))) * gate_ref[:, 0:1]
    for k in range(1, TOP_K):
        f = f + _unpack_bf16_pairs(_load_slabs(y_refs, (k,))) * gate_ref[:, k:k + 1]
    o_ref[...] = h_ref[...] + _rms(f, g_ref[...])


def _combine(y_slabs, gates, h, g_post):
    t = h.shape[0]
    tm = min(ROW_TILE, t)
    return pl.pallas_call(
        _combine_kernel,
        grid=(t // tm,),
        in_specs=[pl.BlockSpec((TOP_K, tm, SLAB), lambda i: (0, i, 0))] * ROW_SPLIT + [
            pl.BlockSpec((tm, TOP_K), lambda i: (i, 0)),
            pl.BlockSpec((tm, D_MODEL), lambda i: (i, 0)),
            pl.BlockSpec((1, D_MODEL), lambda i: (0, 0)),
        ],
        out_specs=pl.BlockSpec((tm, D_MODEL), lambda i: (i, 0)),
        out_shape=jax.ShapeDtypeStruct((t, D_MODEL), F32),
        compiler_params=_params(("parallel",)),
        name="moe_combine",
    )(*y_slabs, gates, h, g_post.reshape(1, D_MODEL))


def _route(logits, bm):
    t = logits.shape[0]
    lane = lax.broadcasted_iota(jnp.int32, logits.shape, 1)
    i1 = jnp.argmax(logits, axis=-1).astype(jnp.int32)
    l1 = jnp.max(logits, axis=-1)
    rest = jnp.where(lane == i1[:, None], -jnp.inf, logits)
    i2 = jnp.argmax(rest, axis=-1).astype(jnp.int32)
    l2 = jnp.max(rest, axis=-1)
    gates = jax.nn.softmax(jnp.stack([l1, l2], axis=-1), axis=-1)
    e_flat = jnp.stack([i1, i2], axis=-1).reshape(t * TOP_K)
    onehot = (e_flat[:, None] == jnp.arange(N_EXPERTS, dtype=jnp.int32)[None, :]).astype(jnp.int32)
    csum = jnp.cumsum(onehot, axis=0)
    rank = jnp.sum((csum - onehot) * onehot, axis=-1)
    counts = csum[-1]
    padded = (counts + bm - 1) // bm * bm
    pad_end = jnp.cumsum(padded)
    pad_start = pad_end - padded
    dest = (pad_start[e_flat] + rank).astype(jnp.int32).reshape(t, TOP_K)
    n_blocks = (t * TOP_K) // bm + N_EXPERTS
    n_used = (pad_end[-1] // bm).astype(jnp.int32)
    blk = jnp.arange(n_blocks, dtype=jnp.int32)
    last = jnp.minimum(blk, n_used - 1) * bm
    block_expert = jnp.minimum(jnp.searchsorted(pad_end, last, side="right"), N_EXPERTS - 1).astype(jnp.int32)
    n_valid = jnp.clip(pad_start[block_expert] + counts[block_expert] - blk * bm, 0, bm)
    n_valid = jnp.where(blk < n_used, n_valid, 0).astype(jnp.int32)
    return dest, gates, block_expert, n_used.reshape(1), n_valid, n_blocks * bm


def kernel(x, mix_norm_pre, mix_norm_post, ffn_norm_pre, ffn_norm_post, w_in, pool_w, pool_scale, na_rpb,
           w_out, dense_w_gate, dense_w_up, dense_w_down, moe_router, moe_w_gate, moe_w_up, moe_w_down):
    b, s, d = x.shape
    assert d == D_MODEL and s % GRID_W == 0
    t = b * s
    rows = s // GRID_W
    depth = w_in.shape[0]
    h = x.reshape(t, d)
    for layer in range(depth):
        j = layer // 2
        is_moe = layer % 2 == 1
        proj = _norm_proj(h, mix_norm_pre[layer], w_in[layer].astype(BF16))
        y_pool = _pool_mixer(proj.reshape(b, s, D_IN), pool_w[layer].astype(BF16), pool_scale[layer])
        y_na = _neighbourhood_attention(proj.reshape(b, rows, GRID_W, D_IN), _na_bias_table(na_rpb[layer]))
        router = None
        if is_moe:
            router = jnp.pad(moe_router[j], ((0, 0), (0, LANES - N_EXPERTS))).astype(BF16)
        outs = _out_proj(y_pool.reshape(t, D_POOL), y_na.reshape(t, D_ATTN), w_out[layer].astype(BF16), h,
                         mix_norm_post[layer], ffn_norm_pre[layer], router)
        if is_moe:
            h1, logits = outs[:2]
            dest, gates, block_expert, n_used, n_valid, n_rows = _route(logits[:, :N_EXPERTS], min(FFN_ROWS, t))
            xs = _sc_dispatch(list(outs[2:]), dest, n_rows)
            ys = _swiglu(xs, moe_w_gate[j].astype(BF16), moe_w_up[j].astype(BF16),
                         moe_w_down[j].astype(BF16), block_expert, n_used, n_valid)
            by_choice = _sc_gather(list(ys), dest.T.reshape(TOP_K * t))
            h = _combine([y.reshape(TOP_K, t, SLAB) for y in by_choice], gates, h1, ffn_norm_post[layer])
        else:
            h1, hn = outs
            bm = min(FFN_ROWS, t)
            n_blk = t // bm
            h = _swiglu(hn, dense_w_gate[j][None].astype(BF16), dense_w_up[j][None].astype(BF16),
                        dense_w_down[j][None].astype(BF16), jnp.zeros((n_blk,), jnp.int32),
                        jnp.full((1,), n_blk, jnp.int32), jnp.full((n_blk,), bm, jnp.int32),
                        resid=h1, g_post=ffn_norm_post[layer])
    return h.reshape(b, s, d)
```

```python
import functools

import jax
import jax.numpy as jnp
import numpy as np
from jax import lax
from jax.experimental import pallas as pl
from jax.experimental.pallas import tpu as pltpu
from jax.experimental.pallas import tpu_sc as plsc

F32 = jnp.float32
BF16 = jnp.bfloat16

D_MODEL = 1024
D_POOL = 512
POOL_WINDOWS = (2, 4, 8, 16)
POOL_GROUP_DIM = 128
D_ATTN = 512
NA_HEAD_DIM = 32
NA_HEADS = 16
D_IN = D_POOL + 3 * D_ATTN
GRID_W = 64
NA_KH = 8
NA_KW = 16
N_EXPERTS = 8
TOP_K = 2
RMS_EPS = 1e-6
NEG_BIG = -1e30
LOG2E = 1.4426950408889634

LANES = 128
SUBLANES = 8
HEADS_PER_GROUP = 256 // NA_HEAD_DIM
VMEM_LIMIT = 52 * 1024 * 1024

ROW_TILE = 512
POOL_SUB = 128
POOL_HALO = 64
FFN_ROWS = 512
FFN_COLS_MAX = 2048
MXU_TILE = 256
NA_ROWS_PER_STEP = 4
NA_SLOTS = 8
SC_WINDOW = 128
ROW_SPLIT = 2


def _rms(x, g):
    ms = jnp.mean(x * x, axis=-1, keepdims=True)
    return x * lax.rsqrt(ms + RMS_EPS) * g


def _params(sem):
    return pltpu.CompilerParams(dimension_semantics=sem, vmem_limit_bytes=VMEM_LIMIT)


def _norm_proj_kernel(x_ref, g_ref, w_ref, o_ref):
    hn = _rms(x_ref[...], g_ref[...])
    p = jnp.dot(hn.astype(BF16), w_ref[...], preferred_element_type=F32)
    q_lo, q_hi = D_POOL, D_POOL + D_ATTN
    o_ref[:, :q_lo] = p[:, :q_lo].astype(BF16)
    o_ref[:, q_lo:q_hi] = (p[:, q_lo:q_hi] * (NA_HEAD_DIM ** -0.5 * LOG2E)).astype(BF16)
    o_ref[:, q_hi:] = p[:, q_hi:].astype(BF16)


def _norm_proj(h, g, w_bf16):
    t = h.shape[0]
    tm = min(ROW_TILE, t)
    return pl.pallas_call(
        _norm_proj_kernel,
        grid=(t // tm,),
        in_specs=[
            pl.BlockSpec((tm, D_MODEL), lambda i: (i, 0)),
            pl.BlockSpec((1, D_MODEL), lambda i: (0, 0)),
            pl.BlockSpec((D_MODEL, D_IN), lambda i: (0, 0)),
        ],
        out_specs=pl.BlockSpec((tm, D_IN), lambda i: (i, 0)),
        out_shape=jax.ShapeDtypeStruct((t, D_IN), BF16),
        compiler_params=_params(("parallel",)),
        name="norm_proj",
    )(h, g.reshape(1, D_MODEL), w_bf16)


def _pool_bands():
    kdim = POOL_SUB + 2 * POOL_HALO
    rel = np.arange(kdim)[None, :] - np.arange(POOL_SUB)[:, None] - POOL_HALO
    return np.stack([((rel >= -(w // 2)) & (rel < w - w // 2)) for w in POOL_WINDOWS]).astype(np.float32)


def _pool_kernel(seq_len, ts, cur_ref, prev_ref, next_ref, band_ref, pw_ref, ps_ref, o_ref, win_ref):
    i = pl.program_id(1)
    t0 = i * ts
    halo_zero = jnp.zeros((POOL_HALO, D_POOL), BF16)
    win_ref[:POOL_HALO, :] = jnp.where(i == 0, halo_zero, prev_ref[0])
    win_ref[POOL_HALO:POOL_HALO + ts, :] = cur_ref[0]
    win_ref[POOL_HALO + ts:, :] = jnp.where(i == pl.num_programs(1) - 1, halo_zero, next_ref[0])
    kdim = POOL_SUB + 2 * POOL_HALO
    t_abs = t0 + lax.broadcasted_iota(jnp.int32, (ts, 1), 0)
    for g, w in enumerate(POOL_WINDOWS):
        half = w // 2
        c0, c1 = g * POOL_GROUP_DIM, (g + 1) * POOL_GROUP_DIM
        cnt = (jnp.minimum(t_abs + (w - half), seq_len) - jnp.maximum(t_abs - half, 0)).astype(F32)
        wsum = jnp.concatenate(
            [jnp.dot(band_ref[g], win_ref[s * POOL_SUB:s * POOL_SUB + kdim, c0:c1], preferred_element_type=F32)
             for s in range(ts // POOL_SUB)], axis=0)
        delta = wsum / cnt - cur_ref[0, :, c0:c1].astype(F32)
        y = jnp.dot(delta.astype(BF16), pw_ref[g], preferred_element_type=F32)
        o_ref[0, :, c0:c1] = (y * ps_ref[:, c0:c1]).astype(BF16)


def _pool_mixer(proj3, pool_w_bf16, pool_scale):
    b, s, _ = proj3.shape
    ts = min(ROW_TILE, s)
    hb = ts // POOL_HALO
    n_halo = s // POOL_HALO
    bands = jnp.asarray(_pool_bands(), BF16)
    return pl.pallas_call(
        functools.partial(_pool_kernel, s, ts),
        grid=(b, s // ts),
        in_specs=[
            pl.BlockSpec((1, ts, D_POOL), lambda bi, i: (bi, i, 0)),
            pl.BlockSpec((1, POOL_HALO, D_POOL), lambda bi, i: (bi, jnp.maximum(i * hb - 1, 0), 0)),
            pl.BlockSpec((1, POOL_HALO, D_POOL),
                         lambda bi, i: (bi, jnp.minimum((i + 1) * hb, n_halo - 1), 0)),
            pl.BlockSpec(bands.shape, lambda bi, i: (0, 0, 0)),
            pl.BlockSpec((len(POOL_WINDOWS), POOL_GROUP_DIM, POOL_GROUP_DIM), lambda bi, i: (0, 0, 0)),
            pl.BlockSpec((1, D_POOL), lambda bi, i: (0, 0)),
        ],
        out_specs=pl.BlockSpec((1, ts, D_POOL), lambda bi, i: (bi, i, 0)),
        out_shape=jax.ShapeDtypeStruct((b, s, D_POOL), BF16),
        scratch_shapes=[pltpu.VMEM((ts + 2 * POOL_HALO, D_POOL), BF16)],
        compiler_params=_params(("parallel", "parallel")),
        name="pool_mixer",
    )(proj3, proj3, proj3, bands, pool_w_bf16, pool_scale.reshape(1, D_POOL))


def _na_bias_pairs(rpb):
    cols = np.arange(GRID_W)
    c0 = np.clip(cols - NA_KW // 2, 0, GRID_W - NA_KW)
    kc = np.arange(GRID_W)
    inside = (kc[None, :] >= c0[:, None]) & (kc[None, :] < c0[:, None] + NA_KW)
    rp = jnp.pad(rpb.astype(F32), ((0, 0), (0, 0), (GRID_W, GRID_W)))
    per_c = [rp[:, :, GRID_W + NA_KW - 1 - c:2 * GRID_W + NA_KW - 1 - c] for c in range(GRID_W)]
    by_dr = jnp.where(inside[None, None], jnp.stack(per_c, axis=2) * LOG2E, NEG_BIG)
    return jnp.concatenate([by_dr[:, :-1], by_dr[:, 1:]], axis=-1)


def _na_kernel(n_rows, q_ref, k_ref, v_ref, bias_ref, o_ref, s_ref, p_ref):
    n_batch, rows_per_step = q_ref.shape[0], q_ref.shape[1]
    win_rows = k_ref.shape[1]
    gw = HEADS_PER_GROUP * NA_HEAD_DIM
    n_groups = NA_HEADS // HEADS_PER_GROUP
    n_slots = s_ref.shape[0]
    lane_head = lax.broadcasted_iota(jnp.int32, (HEADS_PER_GROUP, 1, gw), 2) // NA_HEAD_DIM
    head_id = lax.broadcasted_iota(jnp.int32, (HEADS_PER_GROUP, 1, gw), 0)
    own = lane_head == head_id
    out_head = lax.broadcasted_iota(jnp.int32, (GRID_W, gw), 1) // NA_HEAD_DIM
    first_row = pl.program_id(0) * rows_per_step
    win0 = jnp.clip(first_row - NA_KH // 2, 0, n_rows - win_rows)
    chain = 0
    for j in range(rows_per_step):
        r = first_row + j
        r0 = jnp.clip(r - NA_KH // 2, 0, n_rows - NA_KH)
        off = r0 - win0
        d = r0 - r + NA_KH - 1
        for bi in range(n_batch):
            for g in range(n_groups):
                slot = chain % n_slots
                chain += 1
                lo, hi = g * gw, (g + 1) * gw
                qg = q_ref[bi, j, :, lo:hi]
                zero = jnp.zeros_like(qg)
                qm = jnp.where(own, qg[None], zero[None]).reshape(HEADS_PER_GROUP * GRID_W, gw)
                kw = k_ref[bi, pl.ds(off, NA_KH), :, lo:hi].reshape(NA_KH * GRID_W, gw)
                vw = v_ref[bi, pl.ds(off, NA_KH), :, lo:hi].reshape(NA_KH * GRID_W, gw)
                s_ref[slot] = lax.dot_general(qm, kw, (((1,), (1,)), ((), ())), preferred_element_type=F32)
                inv_l = []
                for h in range(HEADS_PER_GROUP):
                    rows = slice(h * GRID_W, (h + 1) * GRID_W)
                    bias = jnp.concatenate([bias_ref[g * HEADS_PER_GROUP + h, d + 2 * pair]
                                            for pair in range(NA_KH // 2)], axis=1)
                    s = s_ref[slot, rows, :] + bias
                    p = jnp.exp2(s - jnp.max(s, axis=-1, keepdims=True))
                    inv_l.append(1.0 / jnp.sum(p, axis=-1, keepdims=True))
                    p_ref[slot, rows, :] = p.astype(BF16)
                pv = jnp.dot(p_ref[slot], vw, preferred_element_type=F32)
                out = pv[:GRID_W] * inv_l[0]
                for h in range(1, HEADS_PER_GROUP):
                    out = jnp.where(out_head == h, pv[h * GRID_W:(h + 1) * GRID_W] * inv_l[h], out)
                o_ref[bi, j, :, lo:hi] = out.astype(BF16)


def _neighbourhood_attention(proj4, bias_pairs):
    b, rows, w, _ = proj4.shape
    assert w == GRID_W and rows >= NA_KH
    rps = next(c for c in (NA_ROWS_PER_STEP, 2, 1) if rows % c == 0 and rows >= c + NA_KH - 1)
    win_rows = rps + NA_KH - 1
    n_stack = HEADS_PER_GROUP * GRID_W
    n_keys = NA_KH * GRID_W

    def window(col_block):
        shape = (pl.Element(b), pl.Element(win_rows), pl.Element(GRID_W), pl.Element(D_ATTN))
        return pl.BlockSpec(
            shape, lambda i: (0, jnp.clip(i * rps - NA_KH // 2, 0, rows - win_rows), 0, col_block * D_ATTN))

    return pl.pallas_call(
        functools.partial(_na_kernel, rows),
        grid=(rows // rps,),
        in_specs=[pl.BlockSpec((b, rps, GRID_W, D_ATTN), lambda i: (0, i, 0, 1)), window(2), window(3),
                  pl.BlockSpec(bias_pairs.shape, lambda i: (0, 0, 0, 0))],
        out_specs=pl.BlockSpec((b, rps, GRID_W, D_ATTN), lambda i: (0, i, 0, 0)),
        out_shape=jax.ShapeDtypeStruct((b, rows, GRID_W, D_ATTN), BF16),
        scratch_shapes=[pltpu.VMEM((NA_SLOTS, n_stack, n_keys), F32),
                        pltpu.VMEM((NA_SLOTS, n_stack, n_keys), BF16)],
        compiler_params=_params(("arbitrary",)),
        name="neighbourhood_attention",
    )(proj4, proj4, proj4, bias_pairs)


def _pack_bf16_pairs(x):
    n = x.shape[1] // 2
    hi = lax.bitcast_convert_type(x[:, :n].astype(BF16).astype(F32), jnp.uint32)
    lo = lax.bitcast_convert_type(x[:, n:].astype(BF16).astype(F32), jnp.uint32)
    return hi | (lo >> 16)


def _unpack_bf16_pairs(w):
    hi = lax.bitcast_convert_type(w & jnp.uint32(0xFFFF0000), F32)
    lo = lax.bitcast_convert_type(w << 16, F32)
    return jnp.concatenate([hi, lo], axis=1)


SLAB = D_MODEL // 2 // ROW_SPLIT


def _store_slabs(slab_refs, words):
    for p, ref in enumerate(slab_refs):
        ref[...] = words[:, p * SLAB:(p + 1) * SLAB]


def _load_slabs(slab_refs, lead=()):
    return jnp.concatenate([ref[lead + (slice(None), slice(None))] for ref in slab_refs], axis=1)


ROUTE_LANES = ("choice0", "choice1", "gate0", "gate1", "rank0", "rank1")


def _top2_route(logits, tri_ref, carry_ref):
    lane = lax.broadcasted_iota(jnp.int32, logits.shape, 1).astype(F32)
    valid = jnp.where(lane < N_EXPERTS, logits, -jnp.inf)
    top = []
    for _ in range(TOP_K):
        m = jnp.max(valid, axis=-1, keepdims=True)
        idx = jnp.min(jnp.where(valid == m, lane, float(LANES)), axis=-1, keepdims=True)
        top.append((m, idx))
        valid = jnp.where(lane == idx, -jnp.inf, valid)
    (m0, i0), (m1, i1) = top
    e = jnp.exp(m1 - m0)
    g0 = 1.0 / (1.0 + e)
    g1 = e * g0
    hot0 = (lane == i0).astype(F32)
    hot1 = (lane == i1).astype(F32)
    picked = hot0 + hot1
    earlier = jnp.dot(tri_ref[...], picked.astype(BF16), preferred_element_type=F32) + carry_ref[0:1, :]
    r0 = jnp.sum(earlier * hot0, axis=-1, keepdims=True)
    r1 = jnp.sum(earlier * hot1, axis=-1, keepdims=True)
    record = jnp.zeros_like(logits)
    for pos, val in enumerate((i0, i1, g0, g1, r0, r1)):
        record = jnp.where(lane == pos, val, record)
    counts = carry_ref[0:1, :] + jnp.sum(picked, axis=0, keepdims=True)
    return record, counts


def _out_proj_kernel(with_router, yp_ref, ya_ref, w_ref, h_ref, gpost_ref, gpre_ref, *refs):
    mix = jnp.dot(yp_ref[...], w_ref[:D_POOL, :], preferred_element_type=F32)
    mix = mix + jnp.dot(ya_ref[...], w_ref[D_POOL:, :], preferred_element_type=F32)
    h1 = h_ref[...] + _rms(mix, gpost_ref[...])
    hn = _rms(h1, gpre_ref[...])
    if with_router:
        wr_ref, tri_ref, h1_ref, route_ref, counts_ref = refs[:5]
        slab_refs, carry_ref = refs[5:-1], refs[-1]
        _store_slabs(slab_refs, _pack_bf16_pairs(hn))

        @pl.when(pl.program_id(0) == 0)
        def _():
            carry_ref[...] = jnp.zeros_like(carry_ref)

        logits = jnp.dot(hn.astype(BF16), wr_ref[...], preferred_element_type=F32)
        record, counts = _top2_route(logits, tri_ref, carry_ref)
        route_ref[...] = record
        carry_ref[...] = jnp.broadcast_to(counts, carry_ref.shape)
        counts_ref[...] = jnp.broadcast_to(counts, counts_ref.shape)
    else:
        h1_ref, hn_ref = refs
        hn_ref[...] = hn.astype(BF16)
    h1_ref[...] = h1


def _out_proj(y_pool, y_na, w_out_bf16, h, g_post, g_pre, router_bf16=None):
    t = h.shape[0]
    tm = min(ROW_TILE, t)
    with_router = router_bf16 is not None
    row = lambda width: pl.BlockSpec((tm, width), lambda i: (i, 0))
    full = lambda a, b: pl.BlockSpec((a, b), lambda i: (0, 0))
    in_specs = [row(D_POOL), row(D_ATTN), full(D_MODEL, D_MODEL), row(D_MODEL),
                full(1, D_MODEL), full(1, D_MODEL)]
    args = [y_pool, y_na, w_out_bf16, h, g_post.reshape(1, D_MODEL), g_pre.reshape(1, D_MODEL)]
    scratch = []
    if with_router:
        earlier_tokens = jnp.asarray(np.tril(np.ones((tm, tm), np.float32), -1), BF16)
        in_specs += [full(D_MODEL, LANES), full(tm, tm)]
        args += [router_bf16, earlier_tokens]
        out_specs = [row(D_MODEL), row(LANES), full(SUBLANES, LANES)] + [row(SLAB)] * ROW_SPLIT
        out_shape = [jax.ShapeDtypeStruct((t, D_MODEL), F32), jax.ShapeDtypeStruct((t, LANES), F32),
                     jax.ShapeDtypeStruct((SUBLANES, LANES), F32)]
        out_shape += [jax.ShapeDtypeStruct((t, SLAB), jnp.uint32)] * ROW_SPLIT
        scratch = [pltpu.VMEM((SUBLANES, LANES), F32)]
    else:
        out_specs = [row(D_MODEL), row(D_MODEL)]
        out_shape = [jax.ShapeDtypeStruct((t, D_MODEL), F32), jax.ShapeDtypeStruct((t, D_MODEL), BF16)]
    return pl.pallas_call(
        functools.partial(_out_proj_kernel, with_router),
        grid=(t // tm,),
        in_specs=in_specs,
        out_specs=out_specs,
        out_shape=out_shape,
        scratch_shapes=scratch,
        compiler_params=_params(("arbitrary",) if with_router else ("parallel",)),
        name="out_proj_router" if with_router else "out_proj",
    )(*args)


def _swiglu_kernel(dense, be_ref, nused_ref, nvalid_ref, *refs):
    if dense:
        x_ref, wg_ref, wu_ref, wd_ref, h_ref, g_ref, o_ref, acc_ref = refs
    else:
        x_refs, (wg_ref, wu_ref, wd_ref) = refs[:ROW_SPLIT], refs[ROW_SPLIT:ROW_SPLIT + 3]
        o_refs, acc_ref = refs[ROW_SPLIT + 3:-1], refs[-1]
    j = pl.program_id(0)
    f = pl.program_id(1)
    live = j < nused_ref[0]

    @pl.when(live)
    def _():
        if dense:
            x = x_ref[...]
        else:
            row = lax.broadcasted_iota(jnp.int32, (acc_ref.shape[0], 1), 0)
            words = jnp.where(row < nvalid_ref[j], _load_slabs(x_refs), jnp.uint32(0))
            x = _unpack_bf16_pairs(words).astype(BF16)
        a = jnp.dot(x, wg_ref[0], preferred_element_type=F32)
        b = jnp.dot(x, wu_ref[0], preferred_element_type=F32)
        hmid = (a * jax.nn.sigmoid(a) * b).astype(BF16)
        part = jnp.dot(hmid, wd_ref[0], preferred_element_type=F32)

        @pl.when(f == 0)
        def _():
            acc_ref[...] = part

        @pl.when(f > 0)
        def _():
            acc_ref[...] += part

    @pl.when(f == pl.num_programs(1) - 1)
    def _():
        @pl.when(live)
        def _():
            if dense:
                o_ref[...] = h_ref[...] + _rms(acc_ref[...], g_ref[...])
            else:
                _store_slabs(o_refs, _pack_bf16_pairs(acc_ref[...]))

        @pl.when(jnp.logical_not(live))
        def _():
            for ref in ([o_ref] if dense else o_refs):
                ref[...] = jnp.zeros_like(ref)


def _swiglu(x, wg, wu, wd, block_expert, n_used, n_valid, resid=None, g_post=None):
    dense = resid is not None
    xs = [x] if dense else list(x)
    rows = xs[0].shape[0]
    dff = wg.shape[-1]
    bm = min(FFN_ROWS, rows)
    tf = max(c for c in range(MXU_TILE, min(FFN_COLS_MAX, dff) + 1, MXU_TILE) if dff % c == 0)
    assert rows % bm == 0 and dff % tf == 0
    nf = dff // tf

    def f_eff(j, f, nu):
        return jnp.where(j < nu[0], f, nf - 1)

    row_spec = lambda width: pl.BlockSpec((bm, width), lambda j, f, be, nu, nv: (j, 0))
    in_specs = [row_spec(a.shape[1]) for a in xs] + [
        pl.BlockSpec((1, D_MODEL, tf), lambda j, f, be, nu, nv: (be[j], 0, f_eff(j, f, nu))),
        pl.BlockSpec((1, D_MODEL, tf), lambda j, f, be, nu, nv: (be[j], 0, f_eff(j, f, nu))),
        pl.BlockSpec((1, tf, D_MODEL), lambda j, f, be, nu, nv: (be[j], f_eff(j, f, nu), 0)),
    ]
    args = xs + [wg, wu, wd]
    if dense:
        in_specs += [row_spec(D_MODEL), pl.BlockSpec((1, D_MODEL), lambda j, f, be, nu, nv: (0, 0))]
        args += [resid, g_post.reshape(1, D_MODEL)]
        out_shape, out_specs = jax.ShapeDtypeStruct((rows, D_MODEL), F32), row_spec(D_MODEL)
    else:
        out_shape = [jax.ShapeDtypeStruct((rows, SLAB), jnp.uint32)] * ROW_SPLIT
        out_specs = [row_spec(SLAB)] * ROW_SPLIT
    return pl.pallas_call(
        functools.partial(_swiglu_kernel, dense),
        grid_spec=pltpu.PrefetchScalarGridSpec(
            num_scalar_prefetch=3,
            grid=(rows // bm, nf),
            in_specs=in_specs,
            out_specs=out_specs,
            scratch_shapes=[pltpu.VMEM((bm, D_MODEL), F32)],
        ),
        out_shape=out_shape,
        compiler_params=_params(("parallel", "arbitrary")),
        name="swiglu_dense" if dense else "swiglu_experts",
    )(block_expert, n_used, n_valid, *args)


def _sc_mesh():
    return plsc.VectorSubcoreMesh(core_axis_name="core", subcore_axis_name="subcore")


def _sc_dispatch(slabs, dest, n_rows):
    t, width = slabs[0].shape
    idx = [dest[k].reshape(1, t) for k in range(TOP_K)]
    n_slabs = len(slabs)

    @functools.partial(pl.kernel, mesh=_sc_mesh(), scratch_types=[], name="moe_dispatch_sc",
                       out_type=[jax.ShapeDtypeStruct((n_rows, width), slabs[0].dtype)] * n_slabs)
    def scatter_rows(*refs):
        x_hbm, idx_hbm, o_hbm = refs[:n_slabs], refs[n_slabs:n_slabs + TOP_K], refs[n_slabs + TOP_K:]
        for p in range(n_slabs):
            def body(x_vmem, *idx_vmem, out=o_hbm[p]):
                for i_vmem in idx_vmem:
                    pltpu.sync_copy(x_vmem, out.at[i_vmem.at[0]])

            pltpu.emit_pipeline(
                body,
                grid=(t // SC_WINDOW,),
                in_specs=[pl.BlockSpec((SC_WINDOW, width), lambda i: (i, 0))]
                + [pl.BlockSpec((1, SC_WINDOW), lambda i: (0, i))] * TOP_K,
                out_specs=[],
                core_axis_name=("core", "subcore"),
                dimension_semantics=(pltpu.PARALLEL,),
            )(x_hbm[p], *idx_hbm)

    return scatter_rows(*slabs, *idx)


def _sc_gather(slabs, idx):
    n = idx.shape[0]
    width = slabs[0].shape[1]
    n_slabs = len(slabs)

    @functools.partial(pl.kernel, mesh=_sc_mesh(), scratch_types=[], name="moe_gather_sc",
                       out_type=[jax.ShapeDtypeStruct((n, width), slabs[0].dtype)] * n_slabs)
    def gather_rows(*refs):
        x_hbm, i_hbm, o_hbm = refs[:n_slabs], refs[n_slabs], refs[n_slabs + 1:]
        for p in range(n_slabs):
            def body(i_vmem, o_vmem, src=x_hbm[p]):
                pltpu.sync_copy(src.at[i_vmem.at[0]], o_vmem)

            pltpu.emit_pipeline(
                body,
                grid=(n // SC_WINDOW,),
                in_specs=[pl.BlockSpec((1, SC_WINDOW), lambda i: (0, i))],
                out_specs=[pl.BlockSpec((SC_WINDOW, width), lambda i: (i, 0))],
                core_axis_name=("core", "subcore"),
                dimension_semantics=(pltpu.PARALLEL,),
            )(i_hbm, o_hbm[p])

    return gather_rows(*slabs, idx.reshape(1, n))


def _combine_kernel(*refs):
    y_refs, (gate_ref, h_ref, g_ref, o_ref) = refs[:ROW_SPLIT], refs[ROW_SPLIT:]
    f = _unpack_bf16_pairs(_load_slabs(y_refs, (0,))) * gate_ref[:, 0:1]
    for k in range(1, TOP_K):
        f = f + _unpack_bf16_pairs(_load_slabs(y_refs, (k,))) * gate_ref[:, k:k + 1]
    o_ref[...] = h_ref[...] + _rms(f, g_ref[...])


def _combine(y_slabs, gates, h, g_post):
    t = h.shape[0]
    tm = min(ROW_TILE, t)
    return pl.pallas_call(
        _combine_kernel,
        grid=(t // tm,),
        in_specs=[pl.BlockSpec((TOP_K, tm, SLAB), lambda i: (0, i, 0))] * ROW_SPLIT + [
            pl.BlockSpec((tm, TOP_K), lambda i: (i, 0)),
            pl.BlockSpec((tm, D_MODEL), lambda i: (i, 0)),
            pl.BlockSpec((1, D_MODEL), lambda i: (0, 0)),
        ],
        out_specs=pl.BlockSpec((tm, D_MODEL), lambda i: (i, 0)),
        out_shape=jax.ShapeDtypeStruct((t, D_MODEL), F32),
        compiler_params=_params(("parallel",)),
        name="moe_combine",
    )(*y_slabs, gates, h, g_post.reshape(1, D_MODEL))


def _route(record, counts, bm):
    t = record.shape[0]
    lanes = {name: record[:, pos] for pos, name in enumerate(ROUTE_LANES)}
    counts = counts[0, :N_EXPERTS].astype(jnp.int32)
    padded = (counts + bm - 1) // bm * bm
    pad_end = jnp.cumsum(padded)
    pad_start = pad_end - padded
    dest = []
    for k in range(TOP_K):
        choice = lanes[f"choice{k}"].astype(jnp.int32)
        start = sum(jnp.where(choice == e, pad_start[e], 0) for e in range(N_EXPERTS))
        dest.append(start + lanes[f"rank{k}"].astype(jnp.int32))
    dest = jnp.stack(dest, axis=0)
    gates = jnp.stack([lanes[f"gate{k}"] for k in range(TOP_K)], axis=-1)
    n_blocks = (t * TOP_K) // bm + N_EXPERTS
    n_used = (pad_end[-1] // bm).astype(jnp.int32)
    blk = jnp.arange(n_blocks, dtype=jnp.int32)
    last = jnp.minimum(blk, n_used - 1) * bm
    block_expert = jnp.minimum(jnp.searchsorted(pad_end, last, side="right"), N_EXPERTS - 1).astype(jnp.int32)
    n_valid = jnp.clip(pad_start[block_expert] + counts[block_expert] - blk * bm, 0, bm)
    n_valid = jnp.where(blk < n_used, n_valid, 0).astype(jnp.int32)
    return dest, gates, block_expert, n_used.reshape(1), n_valid, n_blocks * bm


def kernel(x, mix_norm_pre, mix_norm_post, ffn_norm_pre, ffn_norm_post, w_in, pool_w, pool_scale, na_rpb,
           w_out, dense_w_gate, dense_w_up, dense_w_down, moe_router, moe_w_gate, moe_w_up, moe_w_down):
    b, s, d = x.shape
    assert d == D_MODEL and s % GRID_W == 0
    t = b * s
    rows = s // GRID_W
    depth = w_in.shape[0]
    h = x.reshape(t, d)
    for layer in range(depth):
        j = layer // 2
        is_moe = layer % 2 == 1
        proj = _norm_proj(h, mix_norm_pre[layer], w_in[layer].astype(BF16))
        y_pool = _pool_mixer(proj.reshape(b, s, D_IN), pool_w[layer].astype(BF16), pool_scale[layer])
        y_na = _neighbourhood_attention(proj.reshape(b, rows, GRID_W, D_IN), _na_bias_pairs(na_rpb[layer]))
        router = None
        if is_moe:
            router = jnp.pad(moe_router[j], ((0, 0), (0, LANES - N_EXPERTS))).astype(BF16)
        outs = _out_proj(y_pool.reshape(t, D_POOL), y_na.reshape(t, D_ATTN), w_out[layer].astype(BF16), h,
                         mix_norm_post[layer], ffn_norm_pre[layer], router)
        if is_moe:
            h1, record, counts = outs[:3]
            dest, gates, block_expert, n_used, n_valid, n_rows = _route(record, counts, min(FFN_ROWS, t))
            xs = _sc_dispatch(list(outs[3:]), dest, n_rows)
            ys = _swiglu(xs, moe_w_gate[j].astype(BF16), moe_w_up[j].astype(BF16),
                         moe_w_down[j].astype(BF16), block_expert, n_used, n_valid)
            by_choice = _sc_gather(list(ys), dest.reshape(TOP_K * t))
            h = _combine([y.reshape(TOP_K, t, SLAB) for y in by_choice], gates, h1, ffn_norm_post[layer])
        else:
            h1, hn = outs
            bm = min(FFN_ROWS, t)
            n_blk = t // bm
            h = _swiglu(hn, dense_w_gate[j][None].astype(BF16), dense_w_up[j][None].astype(BF16),
                        dense_w_down[j][None].astype(BF16), jnp.zeros((n_blk,), jnp.int32),
                        jnp.full((1,), n_blk, jnp.int32), jnp.full((n_blk,), bm, jnp.int32),
                        resid=h1, g_post=ffn_norm_post[layer])
    return h.reshape(b, s, d)
```

```python
import functools

import jax
import jax.numpy as jnp
import numpy as np
from jax import lax
from jax.experimental import pallas as pl
from jax.experimental.pallas import tpu as pltpu
from jax.experimental.pallas import tpu_sc as plsc

F32 = jnp.float32
BF16 = jnp.bfloat16

D_MODEL = 1024
D_POOL = 512
POOL_WINDOWS = (2, 4, 8, 16)
POOL_GROUP_DIM = 128
D_ATTN = 512
NA_HEAD_DIM = 32
NA_HEADS = 16
D_IN = D_POOL + 3 * D_ATTN
GRID_W = 64
NA_KH = 8
NA_KW = 16
N_EXPERTS = 8
TOP_K = 2
RMS_EPS = 1e-6
NEG_BIG = -1e30
LOG2E = 1.4426950408889634

LANES = 128
SUBLANES = 8
HEADS_PER_GROUP = 256 // NA_HEAD_DIM
VMEM_LIMIT = 52 * 1024 * 1024

ROW_TILE = 1024
POOL_SUB = 128
POOL_HALO = 64
FFN_ROWS = 512
FFN_COLS_MAX = 2048
MXU_TILE = 256
NA_ROWS_PER_STEP = 4
NA_SLOTS = 8
SC_WINDOW = 128
ROW_SPLIT = 2


def _rms(x, g):
    ms = jnp.mean(x * x, axis=-1, keepdims=True)
    return x * lax.rsqrt(ms + RMS_EPS) * g


def _params(sem):
    return pltpu.CompilerParams(dimension_semantics=sem, vmem_limit_bytes=VMEM_LIMIT)


def _norm_proj_kernel(x_ref, g_ref, w_ref, o_ref):
    hn = _rms(x_ref[...], g_ref[...])
    p = jnp.dot(hn.astype(BF16), w_ref[...], preferred_element_type=F32)
    q_lo, q_hi = D_POOL, D_POOL + D_ATTN
    o_ref[:, :q_lo] = p[:, :q_lo].astype(BF16)
    o_ref[:, q_lo:q_hi] = (p[:, q_lo:q_hi] * (NA_HEAD_DIM ** -0.5 * LOG2E)).astype(BF16)
    o_ref[:, q_hi:] = p[:, q_hi:].astype(BF16)


def _norm_proj(h, g, w_bf16):
    t = h.shape[0]
    tm = min(ROW_TILE, t)
    return pl.pallas_call(
        _norm_proj_kernel,
        grid=(t // tm,),
        in_specs=[
            pl.BlockSpec((tm, D_MODEL), lambda i: (i, 0)),
            pl.BlockSpec((1, D_MODEL), lambda i: (0, 0)),
            pl.BlockSpec((D_MODEL, D_IN), lambda i: (0, 0)),
        ],
        out_specs=pl.BlockSpec((tm, D_IN), lambda i: (i, 0)),
        out_shape=jax.ShapeDtypeStruct((t, D_IN), BF16),
        compiler_params=_params(("parallel",)),
        name="norm_proj",
    )(h, g.reshape(1, D_MODEL), w_bf16)


def _pool_bands():
    kdim = POOL_SUB + 2 * POOL_HALO
    rel = np.arange(kdim)[None, :] - np.arange(POOL_SUB)[:, None] - POOL_HALO
    return np.stack([((rel >= -(w // 2)) & (rel < w - w // 2)) for w in POOL_WINDOWS]).astype(np.float32)


def _pool_kernel(seq_len, ts, cur_ref, prev_ref, next_ref, band_ref, pw_ref, ps_ref, o_ref, win_ref):
    i = pl.program_id(1)
    t0 = i * ts
    halo_zero = jnp.zeros((POOL_HALO, D_POOL), BF16)
    win_ref[:POOL_HALO, :] = jnp.where(i == 0, halo_zero, prev_ref[0])
    win_ref[POOL_HALO:POOL_HALO + ts, :] = cur_ref[0]
    win_ref[POOL_HALO + ts:, :] = jnp.where(i == pl.num_programs(1) - 1, halo_zero, next_ref[0])
    kdim = POOL_SUB + 2 * POOL_HALO
    t_abs = t0 + lax.broadcasted_iota(jnp.int32, (ts, 1), 0)
    for g, w in enumerate(POOL_WINDOWS):
        half = w // 2
        c0, c1 = g * POOL_GROUP_DIM, (g + 1) * POOL_GROUP_DIM
        cnt = (jnp.minimum(t_abs + (w - half), seq_len) - jnp.maximum(t_abs - half, 0)).astype(F32)
        wsum = jnp.concatenate(
            [jnp.dot(band_ref[g], win_ref[s * POOL_SUB:s * POOL_SUB + kdim, c0:c1], preferred_element_type=F32)
             for s in range(ts // POOL_SUB)], axis=0)
        delta = wsum / cnt - cur_ref[0, :, c0:c1].astype(F32)
        y = jnp.dot(delta.astype(BF16), pw_ref[g], preferred_element_type=F32)
        o_ref[0, :, c0:c1] = (y * ps_ref[:, c0:c1]).astype(BF16)


def _pool_mixer(proj3, pool_w_bf16, pool_scale):
    b, s, _ = proj3.shape
    ts = min(ROW_TILE, s)
    hb = ts // POOL_HALO
    n_halo = s // POOL_HALO
    bands = jnp.asarray(_pool_bands(), BF16)
    return pl.pallas_call(
        functools.partial(_pool_kernel, s, ts),
        grid=(b, s // ts),
        in_specs=[
            pl.BlockSpec((1, ts, D_POOL), lambda bi, i: (bi, i, 0)),
            pl.BlockSpec((1, POOL_HALO, D_POOL), lambda bi, i: (bi, jnp.maximum(i * hb - 1, 0), 0)),
            pl.BlockSpec((1, POOL_HALO, D_POOL),
                         lambda bi, i: (bi, jnp.minimum((i + 1) * hb, n_halo - 1), 0)),
            pl.BlockSpec(bands.shape, lambda bi, i: (0, 0, 0)),
            pl.BlockSpec((len(POOL_WINDOWS), POOL_GROUP_DIM, POOL_GROUP_DIM), lambda bi, i: (0, 0, 0)),
            pl.BlockSpec((1, D_POOL), lambda bi, i: (0, 0)),
        ],
        out_specs=pl.BlockSpec((1, ts, D_POOL), lambda bi, i: (bi, i, 0)),
        out_shape=jax.ShapeDtypeStruct((b, s, D_POOL), BF16),
        scratch_shapes=[pltpu.VMEM((ts + 2 * POOL_HALO, D_POOL), BF16)],
        compiler_params=_params(("parallel", "parallel")),
        name="pool_mixer",
    )(proj3, proj3, proj3, bands, pool_w_bf16, pool_scale.reshape(1, D_POOL))


def _na_bias_pairs(rpb):
    cols = np.arange(GRID_W)
    c0 = np.clip(cols - NA_KW // 2, 0, GRID_W - NA_KW)
    kc = np.arange(GRID_W)
    inside = (kc[None, :] >= c0[:, None]) & (kc[None, :] < c0[:, None] + NA_KW)
    rp = jnp.pad(rpb.astype(F32), ((0, 0), (0, 0), (GRID_W, GRID_W)))
    span = 2 * GRID_W
    u = rp[..., NA_KW:NA_KW + span]
    skew = jnp.tile(u, (1, 1, GRID_W + 1))[..., :GRID_W * (span + 1)]
    skew = skew.reshape(rpb.shape[0], rpb.shape[1], GRID_W, span + 1)[..., :GRID_W]
    by_dr = jnp.where(inside[None, None], jnp.flip(skew, axis=2) * LOG2E, NEG_BIG)
    return jnp.concatenate([by_dr[:, :-1], by_dr[:, 1:]], axis=-1)


def _na_kernel(n_rows, q_ref, k_ref, v_ref, bias_ref, o_ref, s_ref, p_ref):
    n_batch, rows_per_step = q_ref.shape[0], q_ref.shape[1]
    win_rows = k_ref.shape[1]
    gw = HEADS_PER_GROUP * NA_HEAD_DIM
    n_groups = NA_HEADS // HEADS_PER_GROUP
    n_slots = s_ref.shape[0]
    lane_head = lax.broadcasted_iota(jnp.int32, (HEADS_PER_GROUP, 1, gw), 2) // NA_HEAD_DIM
    head_id = lax.broadcasted_iota(jnp.int32, (HEADS_PER_GROUP, 1, gw), 0)
    own = lane_head == head_id
    out_head = lax.broadcasted_iota(jnp.int32, (GRID_W, gw), 1) // NA_HEAD_DIM
    first_row = pl.program_id(0) * rows_per_step
    win0 = jnp.clip(first_row - NA_KH // 2, 0, n_rows - win_rows)
    chain = 0
    for j in range(rows_per_step):
        r = first_row + j
        r0 = jnp.clip(r - NA_KH // 2, 0, n_rows - NA_KH)
        off = r0 - win0
        d = r0 - r + NA_KH - 1
        for bi in range(n_batch):
            for g in range(n_groups):
                slot = chain % n_slots
                chain += 1
                lo, hi = g * gw, (g + 1) * gw
                qg = q_ref[bi, j, :, lo:hi]
                zero = jnp.zeros_like(qg)
                qm = jnp.where(own, qg[None], zero[None]).reshape(HEADS_PER_GROUP * GRID_W, gw)
                kw = k_ref[bi, pl.ds(off, NA_KH), :, lo:hi].reshape(NA_KH * GRID_W, gw)
                vw = v_ref[bi, pl.ds(off, NA_KH), :, lo:hi].reshape(NA_KH * GRID_W, gw)
                s_ref[slot] = lax.dot_general(qm, kw, (((1,), (1,)), ((), ())), preferred_element_type=F32)
                inv_l = []
                for h in range(HEADS_PER_GROUP):
                    rows = slice(h * GRID_W, (h + 1) * GRID_W)
                    bias = jnp.concatenate([bias_ref[g * HEADS_PER_GROUP + h, d + 2 * pair]
                                            for pair in range(NA_KH // 2)], axis=1)
                    s = s_ref[slot, rows, :] + bias
                    p = jnp.exp2(s - jnp.max(s, axis=-1, keepdims=True))
                    inv_l.append(1.0 / jnp.sum(p, axis=-1, keepdims=True))
                    p_ref[slot, rows, :] = p.astype(BF16)
                pv = jnp.dot(p_ref[slot], vw, preferred_element_type=F32)
                out = pv[:GRID_W] * inv_l[0]
                for h in range(1, HEADS_PER_GROUP):
                    out = jnp.where(out_head == h, pv[h * GRID_W:(h + 1) * GRID_W] * inv_l[h], out)
                o_ref[bi, j, :, lo:hi] = out.astype(BF16)


def _neighbourhood_attention(proj4, bias_pairs):
    b, rows, w, _ = proj4.shape
    assert w == GRID_W and rows >= NA_KH
    rps = next(c for c in (NA_ROWS_PER_STEP, 2, 1) if rows % c == 0 and rows >= c + NA_KH - 1)
    win_rows = rps + NA_KH - 1
    n_stack = HEADS_PER_GROUP * GRID_W
    n_keys = NA_KH * GRID_W

    def window(col_block):
        shape = (pl.Element(b), pl.Element(win_rows), pl.Element(GRID_W), pl.Element(D_ATTN))
        return pl.BlockSpec(
            shape, lambda i: (0, jnp.clip(i * rps - NA_KH // 2, 0, rows - win_rows), 0, col_block * D_ATTN))

    return pl.pallas_call(
        functools.partial(_na_kernel, rows),
        grid=(rows // rps,),
        in_specs=[pl.BlockSpec((b, rps, GRID_W, D_ATTN), lambda i: (0, i, 0, 1)), window(2), window(3),
                  pl.BlockSpec(bias_pairs.shape, lambda i: (0, 0, 0, 0))],
        out_specs=pl.BlockSpec((b, rps, GRID_W, D_ATTN), lambda i: (0, i, 0, 0)),
        out_shape=jax.ShapeDtypeStruct((b, rows, GRID_W, D_ATTN), BF16),
        scratch_shapes=[pltpu.VMEM((NA_SLOTS, n_stack, n_keys), F32),
                        pltpu.VMEM((NA_SLOTS, n_stack, n_keys), BF16)],
        compiler_params=_params(("arbitrary",)),
        name="neighbourhood_attention",
    )(proj4, proj4, proj4, bias_pairs)


def _pack_bf16_pairs(x):
    n = x.shape[1] // 2
    hi = lax.bitcast_convert_type(x[:, :n].astype(BF16).astype(F32), jnp.uint32)
    lo = lax.bitcast_convert_type(x[:, n:].astype(BF16).astype(F32), jnp.uint32)
    return hi | (lo >> 16)


def _unpack_bf16_pairs(w):
    hi = lax.bitcast_convert_type(w & jnp.uint32(0xFFFF0000), F32)
    lo = lax.bitcast_convert_type(w << 16, F32)
    return jnp.concatenate([hi, lo], axis=1)


SLAB = D_MODEL // 2 // ROW_SPLIT


def _store_slabs(slab_refs, words):
    for p, ref in enumerate(slab_refs):
        ref[...] = words[:, p * SLAB:(p + 1) * SLAB]


def _load_slabs(slab_refs, lead=()):
    return jnp.concatenate([ref[lead + (slice(None), slice(None))] for ref in slab_refs], axis=1)


ROUTE_ROWS = ("choice0", "choice1", "gate0", "gate1", "rank0", "rank1")
EXPERT_ROWS = 16


def _top2_route(logits_t, earlier_ref, carry_ref):
    eid = lax.broadcasted_iota(jnp.int32, logits_t.shape, 0).astype(F32)
    valid = jnp.where(eid < N_EXPERTS, logits_t, -jnp.inf)
    top = []
    for _ in range(TOP_K):
        m = jnp.max(valid, axis=0, keepdims=True)
        idx = jnp.min(jnp.where(valid == m, eid, float(EXPERT_ROWS)), axis=0, keepdims=True)
        top.append((m, idx))
        valid = jnp.where(eid == idx, -jnp.inf, valid)
    (m0, i0), (m1, i1) = top
    e = jnp.exp(m1 - m0)
    g0 = 1.0 / (1.0 + e)
    g1 = e * g0
    hot0 = (eid == i0).astype(F32)
    hot1 = (eid == i1).astype(F32)
    picked = hot0 + hot1
    carry = carry_ref[:, 0:1]
    earlier = jnp.dot(picked.astype(BF16), earlier_ref[...], preferred_element_type=F32) + carry
    r0 = jnp.sum(earlier * hot0, axis=0, keepdims=True)
    r1 = jnp.sum(earlier * hot1, axis=0, keepdims=True)
    row = lax.broadcasted_iota(jnp.int32, (SUBLANES, logits_t.shape[1]), 0)
    record = jnp.zeros((SUBLANES, logits_t.shape[1]), F32)
    for pos, val in enumerate((i0, i1, g0, g1, r0, r1)):
        record = jnp.where(row == pos, val, record)
    return record, carry + jnp.sum(picked, axis=1, keepdims=True)


def _out_proj_kernel(with_router, yp_ref, ya_ref, w_ref, h_ref, gpost_ref, gpre_ref, *refs):
    mix = jnp.dot(yp_ref[...], w_ref[:D_POOL, :], preferred_element_type=F32)
    mix = mix + jnp.dot(ya_ref[...], w_ref[D_POOL:, :], preferred_element_type=F32)
    h1 = h_ref[...] + _rms(mix, gpost_ref[...])
    hn = _rms(h1, gpre_ref[...])
    if with_router:
        wr_ref, earlier_ref, h1_ref, route_ref, counts_ref = refs[:5]
        slab_refs, carry_ref = refs[5:-1], refs[-1]
        _store_slabs(slab_refs, _pack_bf16_pairs(hn))

        @pl.when(pl.program_id(0) == 0)
        def _():
            carry_ref[...] = jnp.zeros_like(carry_ref)

        logits_t = lax.dot_general(wr_ref[...], hn.astype(BF16), (((1,), (1,)), ((), ())),
                                   preferred_element_type=F32)
        record, counts = _top2_route(logits_t, earlier_ref, carry_ref)
        route_ref[...] = record
        carry_ref[...] = jnp.broadcast_to(counts, carry_ref.shape)
        counts_ref[...] = jnp.broadcast_to(counts, counts_ref.shape)
    else:
        h1_ref, hn_ref = refs
        hn_ref[...] = hn.astype(BF16)
    h1_ref[...] = h1


def _out_proj(y_pool, y_na, w_out_bf16, h, g_post, g_pre, router_bf16=None):
    t = h.shape[0]
    tm = min(ROW_TILE, t)
    with_router = router_bf16 is not None
    row = lambda width: pl.BlockSpec((tm, width), lambda i: (i, 0))
    full = lambda a, b: pl.BlockSpec((a, b), lambda i: (0, 0))
    in_specs = [row(D_POOL), row(D_ATTN), full(D_MODEL, D_MODEL), row(D_MODEL),
                full(1, D_MODEL), full(1, D_MODEL)]
    args = [y_pool, y_na, w_out_bf16, h, g_post.reshape(1, D_MODEL), g_pre.reshape(1, D_MODEL)]
    scratch = []
    if with_router:
        earlier_tokens = jnp.asarray(np.triu(np.ones((tm, tm), np.float32), 1), BF16)
        in_specs += [full(EXPERT_ROWS, D_MODEL), full(tm, tm)]
        args += [router_bf16, earlier_tokens]
        out_specs = [row(D_MODEL), pl.BlockSpec((SUBLANES, tm), lambda i: (0, i)), full(EXPERT_ROWS, LANES)]
        out_specs += [row(SLAB)] * ROW_SPLIT
        out_shape = [jax.ShapeDtypeStruct((t, D_MODEL), F32), jax.ShapeDtypeStruct((SUBLANES, t), F32),
                     jax.ShapeDtypeStruct((EXPERT_ROWS, LANES), F32)]
        out_shape += [jax.ShapeDtypeStruct((t, SLAB), jnp.uint32)] * ROW_SPLIT
        scratch = [pltpu.VMEM((EXPERT_ROWS, LANES), F32)]
    else:
        out_specs = [row(D_MODEL), row(D_MODEL)]
        out_shape = [jax.ShapeDtypeStruct((t, D_MODEL), F32), jax.ShapeDtypeStruct((t, D_MODEL), BF16)]
    return pl.pallas_call(
        functools.partial(_out_proj_kernel, with_router),
        grid=(t // tm,),
        in_specs=in_specs,
        out_specs=out_specs,
        out_shape=out_shape,
        scratch_shapes=scratch,
        compiler_params=_params(("arbitrary",) if with_router else ("parallel",)),
        name="out_proj_router" if with_router else "out_proj",
    )(*args)


def _swiglu_kernel(dense, be_ref, nused_ref, nvalid_ref, *refs):
    if dense:
        x_ref, wg_ref, wu_ref, wd_ref, h_ref, g_ref, o_ref, acc_ref = refs
    else:
        x_refs, (wg_ref, wu_ref, wd_ref) = refs[:ROW_SPLIT], refs[ROW_SPLIT:ROW_SPLIT + 3]
        o_refs, acc_ref = refs[ROW_SPLIT + 3:-1], refs[-1]
    j = pl.program_id(0)
    f = pl.program_id(1)
    live = j < nused_ref[0]

    @pl.when(live)
    def _():
        if dense:
            x = x_ref[...]
        else:
            row = lax.broadcasted_iota(jnp.int32, (acc_ref.shape[0], 1), 0)
            words = jnp.where(row < nvalid_ref[j], _load_slabs(x_refs), jnp.uint32(0))
            x = _unpack_bf16_pairs(words).astype(BF16)
        a = jnp.dot(x, wg_ref[0], preferred_element_type=F32)
        b = jnp.dot(x, wu_ref[0], preferred_element_type=F32)
        hmid = (a * jax.nn.sigmoid(a) * b).astype(BF16)
        part = jnp.dot(hmid, wd_ref[0], preferred_element_type=F32)

        @pl.when(f == 0)
        def _():
            acc_ref[...] = part

        @pl.when(f > 0)
        def _():
            acc_ref[...] += part

    @pl.when(f == pl.num_programs(1) - 1)
    def _():
        @pl.when(live)
        def _():
            if dense:
                o_ref[...] = h_ref[...] + _rms(acc_ref[...], g_ref[...])
            else:
                _store_slabs(o_refs, _pack_bf16_pairs(acc_ref[...]))

        @pl.when(jnp.logical_not(live))
        def _():
            for ref in ([o_ref] if dense else o_refs):
                ref[...] = jnp.zeros_like(ref)


def _swiglu(x, wg, wu, wd, block_expert, n_used, n_valid, resid=None, g_post=None):
    dense = resid is not None
    xs = [x] if dense else list(x)
    rows = xs[0].shape[0]
    dff = wg.shape[-1]
    bm = min(FFN_ROWS, rows)
    tf = max(c for c in range(MXU_TILE, min(FFN_COLS_MAX, dff) + 1, MXU_TILE) if dff % c == 0)
    assert rows % bm == 0 and dff % tf == 0
    nf = dff // tf

    def f_eff(j, f, nu):
        return jnp.where(j < nu[0], f, nf - 1)

    row_spec = lambda width: pl.BlockSpec((bm, width), lambda j, f, be, nu, nv: (j, 0))
    in_specs = [row_spec(a.shape[1]) for a in xs] + [
        pl.BlockSpec((1, D_MODEL, tf), lambda j, f, be, nu, nv: (be[j], 0, f_eff(j, f, nu))),
        pl.BlockSpec((1, D_MODEL, tf), lambda j, f, be, nu, nv: (be[j], 0, f_eff(j, f, nu))),
        pl.BlockSpec((1, tf, D_MODEL), lambda j, f, be, nu, nv: (be[j], f_eff(j, f, nu), 0)),
    ]
    args = xs + [wg, wu, wd]
    if dense:
        in_specs += [row_spec(D_MODEL), pl.BlockSpec((1, D_MODEL), lambda j, f, be, nu, nv: (0, 0))]
        args += [resid, g_post.reshape(1, D_MODEL)]
        out_shape, out_specs = jax.ShapeDtypeStruct((rows, D_MODEL), F32), row_spec(D_MODEL)
    else:
        out_shape = [jax.ShapeDtypeStruct((rows, SLAB), jnp.uint32)] * ROW_SPLIT
        out_specs = [row_spec(SLAB)] * ROW_SPLIT
    return pl.pallas_call(
        functools.partial(_swiglu_kernel, dense),
        grid_spec=pltpu.PrefetchScalarGridSpec(
            num_scalar_prefetch=3,
            grid=(rows // bm, nf),
            in_specs=in_specs,
            out_specs=out_specs,
            scratch_shapes=[pltpu.VMEM((bm, D_MODEL), F32)],
        ),
        out_shape=out_shape,
        compiler_params=_params(("parallel", "arbitrary")),
        name="swiglu_dense" if dense else "swiglu_experts",
    )(block_expert, n_used, n_valid, *args)


def _sc_mesh():
    return plsc.VectorSubcoreMesh(core_axis_name="core", subcore_axis_name="subcore")


def _sc_dispatch(slabs, dest, n_rows):
    t, width = slabs[0].shape
    idx = [dest[k].reshape(1, t) for k in range(TOP_K)]
    n_slabs = len(slabs)

    @functools.partial(pl.kernel, mesh=_sc_mesh(), scratch_types=[], name="moe_dispatch_sc",
                       out_type=[jax.ShapeDtypeStruct((n_rows, width), slabs[0].dtype)] * n_slabs)
    def scatter_rows(*refs):
        x_hbm, idx_hbm, o_hbm = refs[:n_slabs], refs[n_slabs:n_slabs + TOP_K], refs[n_slabs + TOP_K:]
        for p in range(n_slabs):
            def body(x_vmem, *idx_vmem, out=o_hbm[p]):
                for i_vmem in idx_vmem:
                    pltpu.sync_copy(x_vmem, out.at[i_vmem.at[0]])

            pltpu.emit_pipeline(
                body,
                grid=(t // SC_WINDOW,),
                in_specs=[pl.BlockSpec((SC_WINDOW, width), lambda i: (i, 0))]
                + [pl.BlockSpec((1, SC_WINDOW), lambda i: (0, i))] * TOP_K,
                out_specs=[],
                core_axis_name=("core", "subcore"),
                dimension_semantics=(pltpu.PARALLEL,),
            )(x_hbm[p], *idx_hbm)

    return scatter_rows(*slabs, *idx)


def _sc_gather(slabs, idx):
    n = idx.shape[0]
    width = slabs[0].shape[1]
    n_slabs = len(slabs)

    @functools.partial(pl.kernel, mesh=_sc_mesh(), scratch_types=[], name="moe_gather_sc",
                       out_type=[jax.ShapeDtypeStruct((n, width), slabs[0].dtype)] * n_slabs)
    def gather_rows(*refs):
        x_hbm, i_hbm, o_hbm = refs[:n_slabs], refs[n_slabs], refs[n_slabs + 1:]
        for p in range(n_slabs):
            def body(i_vmem, o_vmem, src=x_hbm[p]):
                pltpu.sync_copy(src.at[i_vmem.at[0]], o_vmem)

            pltpu.emit_pipeline(
                body,
                grid=(n // SC_WINDOW,),
                in_specs=[pl.BlockSpec((1, SC_WINDOW), lambda i: (0, i))],
                out_specs=[pl.BlockSpec((SC_WINDOW, width), lambda i: (i, 0))],
                core_axis_name=("core", "subcore"),
                dimension_semantics=(pltpu.PARALLEL,),
            )(i_hbm, o_hbm[p])

    return gather_rows(*slabs, idx.reshape(1, n))


def _combine_kernel(*refs):
    y_refs, (route_ref, h_ref, g_ref, o_ref) = refs[:ROW_SPLIT], refs[ROW_SPLIT:]
    per_token = route_ref[...].T
    gate0 = ROUTE_ROWS.index("gate0")
    f = _unpack_bf16_pairs(_load_slabs(y_refs, (0,))) * per_token[:, gate0:gate0 + 1]
    for k in range(1, TOP_K):
        f = f + _unpack_bf16_pairs(_load_slabs(y_refs, (k,))) * per_token[:, gate0 + k:gate0 + k + 1]
    o_ref[...] = h_ref[...] + _rms(f, g_ref[...])


def _combine(y_slabs, record, h, g_post):
    t = h.shape[0]
    tm = min(ROW_TILE, t)
    return pl.pallas_call(
        _combine_kernel,
        grid=(t // tm,),
        in_specs=[pl.BlockSpec((TOP_K, tm, SLAB), lambda i: (0, i, 0))] * ROW_SPLIT + [
            pl.BlockSpec((SUBLANES, tm), lambda i: (0, i)),
            pl.BlockSpec((tm, D_MODEL), lambda i: (i, 0)),
            pl.BlockSpec((1, D_MODEL), lambda i: (0, 0)),
        ],
        out_specs=pl.BlockSpec((tm, D_MODEL), lambda i: (i, 0)),
        out_shape=jax.ShapeDtypeStruct((t, D_MODEL), F32),
        compiler_params=_params(("parallel",)),
        name="moe_combine",
    )(*y_slabs, record, h, g_post.reshape(1, D_MODEL))


def _route(record, counts, bm):
    t = record.shape[1]
    rows = {name: record[pos] for pos, name in enumerate(ROUTE_ROWS)}
    counts = counts[:N_EXPERTS, 0].astype(jnp.int32)
    padded = (counts + bm - 1) // bm * bm
    pad_end = jnp.cumsum(padded)
    pad_start = pad_end - padded
    dest = []
    for k in range(TOP_K):
        choice = rows[f"choice{k}"].astype(jnp.int32)
        start = sum(jnp.where(choice == e, pad_start[e], 0) for e in range(N_EXPERTS))
        dest.append(start + rows[f"rank{k}"].astype(jnp.int32))
    dest = jnp.stack(dest, axis=0)
    n_blocks = (t * TOP_K) // bm + N_EXPERTS
    n_used = (pad_end[-1] // bm).astype(jnp.int32)
    blk = jnp.arange(n_blocks, dtype=jnp.int32)
    last = jnp.minimum(blk, n_used - 1) * bm
    block_expert = jnp.sum((last[:, None] >= pad_end[None, :]).astype(jnp.int32), axis=1)
    block_expert = jnp.minimum(block_expert, N_EXPERTS - 1)
    n_valid = jnp.clip(pad_start[block_expert] + counts[block_expert] - blk * bm, 0, bm)
    n_valid = jnp.where(blk < n_used, n_valid, 0).astype(jnp.int32)
    return dest, block_expert, n_used.reshape(1), n_valid, n_blocks * bm


def kernel(x, mix_norm_pre, mix_norm_post, ffn_norm_pre, ffn_norm_post, w_in, pool_w, pool_scale, na_rpb,
           w_out, dense_w_gate, dense_w_up, dense_w_down, moe_router, moe_w_gate, moe_w_up, moe_w_down):
    b, s, d = x.shape
    assert d == D_MODEL and s % GRID_W == 0
    t = b * s
    rows = s // GRID_W
    depth = w_in.shape[0]
    h = x.reshape(t, d)
    for layer in range(depth):
        j = layer // 2
        is_moe = layer % 2 == 1
        proj = _norm_proj(h, mix_norm_pre[layer], w_in[layer].astype(BF16))
        y_pool = _pool_mixer(proj.reshape(b, s, D_IN), pool_w[layer].astype(BF16), pool_scale[layer])
        y_na = _neighbourhood_attention(proj.reshape(b, rows, GRID_W, D_IN), _na_bias_pairs(na_rpb[layer]))
        router = None
        if is_moe:
            router = jnp.pad(moe_router[j].T, ((0, EXPERT_ROWS - N_EXPERTS), (0, 0))).astype(BF16)
        outs = _out_proj(y_pool.reshape(t, D_POOL), y_na.reshape(t, D_ATTN), w_out[layer].astype(BF16), h,
                         mix_norm_post[layer], ffn_norm_pre[layer], router)
        if is_moe:
            h1, record, counts = outs[:3]
            dest, block_expert, n_used, n_valid, n_rows = _route(record, counts, min(FFN_ROWS, t))
            xs = _sc_dispatch(list(outs[3:]), dest, n_rows)
            ys = _swiglu(xs, moe_w_gate[j].astype(BF16), moe_w_up[j].astype(BF16),
                         moe_w_down[j].astype(BF16), block_expert, n_used, n_valid)
            by_choice = _sc_gather(list(ys), dest.reshape(TOP_K * t))
            h = _combine([y.reshape(TOP_K, t, SLAB) for y in by_choice], record, h1, ffn_norm_post[layer])
        else:
            h1, hn = outs
            bm = min(FFN_ROWS, t)
            n_blk = t // bm
            h = _swiglu(hn, dense_w_gate[j][None].astype(BF16), dense_w_up[j][None].astype(BF16),
                        dense_w_down[j][None].astype(BF16), jnp.zeros((n_blk,), jnp.int32),
                        jnp.full((1,), n_blk, jnp.int32), jnp.full((n_blk,), bm, jnp.int32),
                        resid=h1, g_post=ffn_norm_post[layer])
    return h.reshape(b, s, d)
```

```python
import functools

import jax
import jax.numpy as jnp
import numpy as np
from jax import lax
from jax.experimental import pallas as pl
from jax.experimental.pallas import tpu as pltpu
from jax.experimental.pallas import tpu_sc as plsc

F32 = jnp.float32
BF16 = jnp.bfloat16

D_MODEL = 1024
D_POOL = 512
POOL_WINDOWS = (2, 4, 8, 16)
POOL_GROUP_DIM = 128
D_ATTN = 512
NA_HEAD_DIM = 32
NA_HEADS = 16
D_IN = D_POOL + 3 * D_ATTN
GRID_W = 64
NA_KH = 8
NA_KW = 16
N_EXPERTS = 8
TOP_K = 2
RMS_EPS = 1e-6
NEG_BIG = -1e30
LOG2E = 1.4426950408889634

LANES = 128
SUBLANES = 8
HEADS_PER_GROUP = 256 // NA_HEAD_DIM
VMEM_LIMIT = 52 * 1024 * 1024

ROW_TILE = 1024
POOL_SUB = 128
POOL_HALO = 64
FFN_ROWS = 512
FFN_ROW_CHUNKS = 2
FFN_COLS_MAX = 2048
MXU_TILE = 256
NA_ROWS_PER_STEP = 4
NA_SLOTS = 8
SC_WINDOW = 128
ROW_SPLIT = 2


def _rms(x, g):
    ms = jnp.mean(x * x, axis=-1, keepdims=True)
    return x * lax.rsqrt(ms + RMS_EPS) * g


def _params(sem):
    return pltpu.CompilerParams(dimension_semantics=sem, vmem_limit_bytes=VMEM_LIMIT)


def _norm_proj_kernel(x_ref, g_ref, w_ref, o_ref):
    hn = _rms(x_ref[...], g_ref[...])
    p = jnp.dot(hn.astype(BF16), w_ref[...], preferred_element_type=F32)
    q_lo, q_hi = D_POOL, D_POOL + D_ATTN
    o_ref[:, :q_lo] = p[:, :q_lo].astype(BF16)
    o_ref[:, q_lo:q_hi] = (p[:, q_lo:q_hi] * (NA_HEAD_DIM ** -0.5 * LOG2E)).astype(BF16)
    o_ref[:, q_hi:] = p[:, q_hi:].astype(BF16)


def _norm_proj(h, g, w_bf16):
    t = h.shape[0]
    tm = min(ROW_TILE, t)
    return pl.pallas_call(
        _norm_proj_kernel,
        grid=(t // tm,),
        in_specs=[
            pl.BlockSpec((tm, D_MODEL), lambda i: (i, 0)),
            pl.BlockSpec((1, D_MODEL), lambda i: (0, 0)),
            pl.BlockSpec((D_MODEL, D_IN), lambda i: (0, 0)),
        ],
        out_specs=pl.BlockSpec((tm, D_IN), lambda i: (i, 0)),
        out_shape=jax.ShapeDtypeStruct((t, D_IN), BF16),
        compiler_params=_params(("parallel",)),
        name="norm_proj",
    )(h, g.reshape(1, D_MODEL), w_bf16)


def _pool_bands():
    kdim = POOL_SUB + 2 * POOL_HALO
    rel = np.arange(kdim)[None, :] - np.arange(POOL_SUB)[:, None] - POOL_HALO
    return np.stack([((rel >= -(w // 2)) & (rel < w - w // 2)) for w in POOL_WINDOWS]).astype(np.float32)


def _pool_kernel(seq_len, ts, cur_ref, prev_ref, next_ref, band_ref, pw_ref, ps_ref, o_ref, win_ref):
    i = pl.program_id(1)
    t0 = i * ts
    halo_zero = jnp.zeros((POOL_HALO, D_POOL), BF16)
    win_ref[:POOL_HALO, :] = jnp.where(i == 0, halo_zero, prev_ref[0])
    win_ref[POOL_HALO:POOL_HALO + ts, :] = cur_ref[0]
    win_ref[POOL_HALO + ts:, :] = jnp.where(i == pl.num_programs(1) - 1, halo_zero, next_ref[0])
    kdim = POOL_SUB + 2 * POOL_HALO
    t_abs = t0 + lax.broadcasted_iota(jnp.int32, (ts, 1), 0)
    for g, w in enumerate(POOL_WINDOWS):
        half = w // 2
        c0, c1 = g * POOL_GROUP_DIM, (g + 1) * POOL_GROUP_DIM
        cnt = (jnp.minimum(t_abs + (w - half), seq_len) - jnp.maximum(t_abs - half, 0)).astype(F32)
        wsum = jnp.concatenate(
            [jnp.dot(band_ref[g], win_ref[s * POOL_SUB:s * POOL_SUB + kdim, c0:c1], preferred_element_type=F32)
             for s in range(ts // POOL_SUB)], axis=0)
        delta = wsum / cnt - cur_ref[0, :, c0:c1].astype(F32)
        y = jnp.dot(delta.astype(BF16), pw_ref[g], preferred_element_type=F32)
        o_ref[0, :, c0:c1] = (y * ps_ref[:, c0:c1]).astype(BF16)


def _pool_mixer(proj3, pool_w_bf16, pool_scale):
    b, s, _ = proj3.shape
    ts = min(ROW_TILE, s)
    hb = ts // POOL_HALO
    n_halo = s // POOL_HALO
    bands = jnp.asarray(_pool_bands(), BF16)
    return pl.pallas_call(
        functools.partial(_pool_kernel, s, ts),
        grid=(b, s // ts),
        in_specs=[
            pl.BlockSpec((1, ts, D_POOL), lambda bi, i: (bi, i, 0)),
            pl.BlockSpec((1, POOL_HALO, D_POOL), lambda bi, i: (bi, jnp.maximum(i * hb - 1, 0), 0)),
            pl.BlockSpec((1, POOL_HALO, D_POOL),
                         lambda bi, i: (bi, jnp.minimum((i + 1) * hb, n_halo - 1), 0)),
            pl.BlockSpec(bands.shape, lambda bi, i: (0, 0, 0)),
            pl.BlockSpec((len(POOL_WINDOWS), POOL_GROUP_DIM, POOL_GROUP_DIM), lambda bi, i: (0, 0, 0)),
            pl.BlockSpec((1, D_POOL), lambda bi, i: (0, 0)),
        ],
        out_specs=pl.BlockSpec((1, ts, D_POOL), lambda bi, i: (bi, i, 0)),
        out_shape=jax.ShapeDtypeStruct((b, s, D_POOL), BF16),
        scratch_shapes=[pltpu.VMEM((ts + 2 * POOL_HALO, D_POOL), BF16)],
        compiler_params=_params(("parallel", "parallel")),
        name="pool_mixer",
    )(proj3, proj3, proj3, bands, pool_w_bf16, pool_scale.reshape(1, D_POOL))


RPB_SHIFT = GRID_W - NA_KW


def _na_build_bias(rpb_ref, bias_ref):
    n_dr = rpb_ref.shape[1]
    c = lax.broadcasted_iota(jnp.int32, (GRID_W, 2 * GRID_W), 0)
    lane = lax.broadcasted_iota(jnp.int32, (GRID_W, 2 * GRID_W), 1)
    kc = lane % GRID_W
    c0 = jnp.clip(c - NA_KW // 2, 0, GRID_W - NA_KW)
    inside = (kc >= c0) & (kc < c0 + NA_KW)
    first_half = lane < GRID_W

    def per_head(h, carry):
        halves = []
        for dr in range(n_dr):
            row = jnp.broadcast_to(rpb_ref[h, dr:dr + 1, :], (GRID_W, 2 * GRID_W)) * LOG2E
            halves.append((pltpu.roll(row, GRID_W + 1, 1, stride=1, stride_axis=0),
                           pltpu.roll(row, 1, 1, stride=1, stride_axis=0)))
        for e in range(n_dr - 1):
            pair = jnp.where(first_half, halves[e][0], halves[e + 1][1])
            bias_ref[h, e] = jnp.where(inside, pair, NEG_BIG)
        return carry

    lax.fori_loop(0, rpb_ref.shape[0], per_head, 0)


def _na_kernel(n_rows, q_ref, k_ref, v_ref, rpb_ref, o_ref, s_ref, p_ref, bias_ref):
    @pl.when(pl.program_id(0) == 0)
    def _():
        _na_build_bias(rpb_ref, bias_ref)

    n_batch, rows_per_step = q_ref.shape[0], q_ref.shape[1]
    win_rows = k_ref.shape[1]
    gw = HEADS_PER_GROUP * NA_HEAD_DIM
    n_groups = NA_HEADS // HEADS_PER_GROUP
    n_slots = s_ref.shape[0]
    lane_head = lax.broadcasted_iota(jnp.int32, (HEADS_PER_GROUP, 1, gw), 2) // NA_HEAD_DIM
    head_id = lax.broadcasted_iota(jnp.int32, (HEADS_PER_GROUP, 1, gw), 0)
    own = lane_head == head_id
    out_head = lax.broadcasted_iota(jnp.int32, (GRID_W, gw), 1) // NA_HEAD_DIM
    first_row = pl.program_id(0) * rows_per_step
    win0 = jnp.clip(first_row - NA_KH // 2, 0, n_rows - win_rows)
    chain = 0
    for j in range(rows_per_step):
        r = first_row + j
        r0 = jnp.clip(r - NA_KH // 2, 0, n_rows - NA_KH)
        off = r0 - win0
        d = r0 - r + NA_KH - 1
        for bi in range(n_batch):
            for g in range(n_groups):
                slot = chain % n_slots
                chain += 1
                lo, hi = g * gw, (g + 1) * gw
                qg = q_ref[bi, j, :, lo:hi]
                zero = jnp.zeros_like(qg)
                qm = jnp.where(own, qg[None], zero[None]).reshape(HEADS_PER_GROUP * GRID_W, gw)
                kw = k_ref[bi, pl.ds(off, NA_KH), :, lo:hi].reshape(NA_KH * GRID_W, gw)
                vw = v_ref[bi, pl.ds(off, NA_KH), :, lo:hi].reshape(NA_KH * GRID_W, gw)
                s_ref[slot] = lax.dot_general(qm, kw, (((1,), (1,)), ((), ())), preferred_element_type=F32)
                inv_l = []
                for h in range(HEADS_PER_GROUP):
                    rows = slice(h * GRID_W, (h + 1) * GRID_W)
                    bias = jnp.concatenate([bias_ref[g * HEADS_PER_GROUP + h, d + 2 * pair]
                                            for pair in range(NA_KH // 2)], axis=1)
                    s = s_ref[slot, rows, :] + bias
                    p = jnp.exp2(s - jnp.max(s, axis=-1, keepdims=True))
                    inv_l.append(1.0 / jnp.sum(p, axis=-1, keepdims=True))
                    p_ref[slot, rows, :] = p.astype(BF16)
                pv = jnp.dot(p_ref[slot], vw, preferred_element_type=F32)
                out = pv[:GRID_W] * inv_l[0]
                for h in range(1, HEADS_PER_GROUP):
                    out = jnp.where(out_head == h, pv[h * GRID_W:(h + 1) * GRID_W] * inv_l[h], out)
                o_ref[bi, j, :, lo:hi] = out.astype(BF16)


def _neighbourhood_attention(proj4, rpb):
    b, rows, w, _ = proj4.shape
    assert w == GRID_W and rows >= NA_KH
    n_heads, n_dr, n_dc = rpb.shape
    rpb_rows = jnp.pad(rpb.astype(F32), ((0, 0), (0, 0), (RPB_SHIFT, 2 * GRID_W - RPB_SHIFT - n_dc)))
    rps = next(c for c in (NA_ROWS_PER_STEP, 2, 1) if rows % c == 0 and rows >= c + NA_KH - 1)
    win_rows = rps + NA_KH - 1
    n_stack = HEADS_PER_GROUP * GRID_W
    n_keys = NA_KH * GRID_W

    def window(col_block):
        shape = (pl.Element(b), pl.Element(win_rows), pl.Element(GRID_W), pl.Element(D_ATTN))
        return pl.BlockSpec(
            shape, lambda i: (0, jnp.clip(i * rps - NA_KH // 2, 0, rows - win_rows), 0, col_block * D_ATTN))

    return pl.pallas_call(
        functools.partial(_na_kernel, rows),
        grid=(rows // rps,),
        in_specs=[pl.BlockSpec((b, rps, GRID_W, D_ATTN), lambda i: (0, i, 0, 1)), window(2), window(3),
                  pl.BlockSpec(rpb_rows.shape, lambda i: (0, 0, 0))],
        out_specs=pl.BlockSpec((b, rps, GRID_W, D_ATTN), lambda i: (0, i, 0, 0)),
        out_shape=jax.ShapeDtypeStruct((b, rows, GRID_W, D_ATTN), BF16),
        scratch_shapes=[pltpu.VMEM((NA_SLOTS, n_stack, n_keys), F32),
                        pltpu.VMEM((NA_SLOTS, n_stack, n_keys), BF16),
                        pltpu.VMEM((n_heads, n_dr - 1, GRID_W, 2 * GRID_W), F32)],
        compiler_params=_params(("arbitrary",)),
        name="neighbourhood_attention",
    )(proj4, proj4, proj4, rpb_rows)


def _pack_bf16_pairs(x):
    n = x.shape[1] // 2
    hi = lax.bitcast_convert_type(x[:, :n].astype(BF16).astype(F32), jnp.uint32)
    lo = lax.bitcast_convert_type(x[:, n:].astype(BF16).astype(F32), jnp.uint32)
    return hi | (lo >> 16)


def _unpack_bf16_pairs(w):
    hi = lax.bitcast_convert_type(w & jnp.uint32(0xFFFF0000), F32)
    lo = lax.bitcast_convert_type(w << 16, F32)
    return jnp.concatenate([hi, lo], axis=1)


SLAB = D_MODEL // 2 // ROW_SPLIT


def _store_slabs(slab_refs, words):
    for p, ref in enumerate(slab_refs):
        ref[...] = words[:, p * SLAB:(p + 1) * SLAB]


def _load_slabs(slab_refs, lead=()):
    return jnp.concatenate([ref[lead + (slice(None), slice(None))] for ref in slab_refs], axis=1)


ROUTE_ROWS = ("choice0", "choice1", "gate0", "gate1", "rank0", "rank1")
EXPERT_ROWS = 16


def _top2_route(logits_t, earlier_ref, carry_ref):
    eid = lax.broadcasted_iota(jnp.int32, logits_t.shape, 0).astype(F32)
    valid = jnp.where(eid < N_EXPERTS, logits_t, -jnp.inf)
    top = []
    for _ in range(TOP_K):
        m = jnp.max(valid, axis=0, keepdims=True)
        idx = jnp.min(jnp.where(valid == m, eid, float(EXPERT_ROWS)), axis=0, keepdims=True)
        top.append((m, idx))
        valid = jnp.where(eid == idx, -jnp.inf, valid)
    (m0, i0), (m1, i1) = top
    e = jnp.exp(m1 - m0)
    g0 = 1.0 / (1.0 + e)
    g1 = e * g0
    hot0 = (eid == i0).astype(F32)
    hot1 = (eid == i1).astype(F32)
    picked = hot0 + hot1
    carry = carry_ref[:, 0:1]
    earlier = jnp.dot(picked.astype(BF16), earlier_ref[...], preferred_element_type=F32) + carry
    r0 = jnp.sum(earlier * hot0, axis=0, keepdims=True)
    r1 = jnp.sum(earlier * hot1, axis=0, keepdims=True)
    row = lax.broadcasted_iota(jnp.int32, (SUBLANES, logits_t.shape[1]), 0)
    record = jnp.zeros((SUBLANES, logits_t.shape[1]), F32)
    for pos, val in enumerate((i0, i1, g0, g1, r0, r1)):
        record = jnp.where(row == pos, val, record)
    return record, carry + jnp.sum(picked, axis=1, keepdims=True)


def _out_proj_kernel(with_router, yp_ref, ya_ref, w_ref, h_ref, gpost_ref, gpre_ref, *refs):
    mix = jnp.dot(yp_ref[...], w_ref[:D_POOL, :], preferred_element_type=F32)
    mix = mix + jnp.dot(ya_ref[...], w_ref[D_POOL:, :], preferred_element_type=F32)
    h1 = h_ref[...] + _rms(mix, gpost_ref[...])
    hn = _rms(h1, gpre_ref[...])
    if with_router:
        wr_ref, earlier_ref, h1_ref, route_ref, counts_ref = refs[:5]
        slab_refs, carry_ref = refs[5:-1], refs[-1]
        _store_slabs(slab_refs, _pack_bf16_pairs(hn))

        @pl.when(pl.program_id(0) == 0)
        def _():
            carry_ref[...] = jnp.zeros_like(carry_ref)

        logits_t = lax.dot_general(wr_ref[...], hn.astype(BF16), (((1,), (1,)), ((), ())),
                                   preferred_element_type=F32)
        record, counts = _top2_route(logits_t, earlier_ref, carry_ref)
        route_ref[...] = record
        carry_ref[...] = jnp.broadcast_to(counts, carry_ref.shape)
        counts_ref[...] = jnp.broadcast_to(counts, counts_ref.shape)
    else:
        h1_ref, hn_ref = refs
        hn_ref[...] = hn.astype(BF16)
    h1_ref[...] = h1


def _out_proj(y_pool, y_na, w_out_bf16, h, g_post, g_pre, router_bf16=None):
    t = h.shape[0]
    tm = min(ROW_TILE, t)
    with_router = router_bf16 is not None
    row = lambda width: pl.BlockSpec((tm, width), lambda i: (i, 0))
    full = lambda a, b: pl.BlockSpec((a, b), lambda i: (0, 0))
    in_specs = [row(D_POOL), row(D_ATTN), full(D_MODEL, D_MODEL), row(D_MODEL),
                full(1, D_MODEL), full(1, D_MODEL)]
    args = [y_pool, y_na, w_out_bf16, h, g_post.reshape(1, D_MODEL), g_pre.reshape(1, D_MODEL)]
    scratch = []
    if with_router:
        earlier_tokens = jnp.asarray(np.triu(np.ones((tm, tm), np.float32), 1), BF16)
        in_specs += [full(EXPERT_ROWS, D_MODEL), full(tm, tm)]
        args += [router_bf16, earlier_tokens]
        out_specs = [row(D_MODEL), pl.BlockSpec((SUBLANES, tm), lambda i: (0, i)), full(EXPERT_ROWS, LANES)]
        out_specs += [row(SLAB)] * ROW_SPLIT
        out_shape = [jax.ShapeDtypeStruct((t, D_MODEL), F32), jax.ShapeDtypeStruct((SUBLANES, t), F32),
                     jax.ShapeDtypeStruct((EXPERT_ROWS, LANES), F32)]
        out_shape += [jax.ShapeDtypeStruct((t, SLAB), jnp.uint32)] * ROW_SPLIT
        scratch = [pltpu.VMEM((EXPERT_ROWS, LANES), F32)]
    else:
        out_specs = [row(D_MODEL), row(D_MODEL)]
        out_shape = [jax.ShapeDtypeStruct((t, D_MODEL), F32), jax.ShapeDtypeStruct((t, D_MODEL), BF16)]
    return pl.pallas_call(
        functools.partial(_out_proj_kernel, with_router),
        grid=(t // tm,),
        in_specs=in_specs,
        out_specs=out_specs,
        out_shape=out_shape,
        scratch_shapes=scratch,
        compiler_params=_params(("arbitrary",) if with_router else ("parallel",)),
        name="out_proj_router" if with_router else "out_proj",
    )(*args)


def _swiglu_kernel(dense, be_ref, nused_ref, nvalid_ref, *refs):
    if dense:
        x_ref, wg_ref, wu_ref, wd_ref, h_ref, g_ref, o_ref, acc_ref = refs
    else:
        x_refs, (wg_ref, wu_ref, wd_ref) = refs[:ROW_SPLIT], refs[ROW_SPLIT:ROW_SPLIT + 3]
        o_refs, acc_ref = refs[ROW_SPLIT + 3:-1], refs[-1]
    j = pl.program_id(0)
    f = pl.program_id(1)
    live = j < nused_ref[0]

    @pl.when((j == 0) & (f == 0))
    def _():
        acc_ref[...] = jnp.zeros_like(acc_ref)

    @pl.when(live)
    def _():
        if dense:
            x = x_ref[...]
        else:
            row = lax.broadcasted_iota(jnp.int32, (acc_ref.shape[0], 1), 0)
            words = jnp.where(row < nvalid_ref[j], _load_slabs(x_refs), jnp.uint32(0))
            x = _unpack_bf16_pairs(words).astype(BF16)
        chunk = x.shape[0] // FFN_ROW_CHUNKS
        for c in range(FFN_ROW_CHUNKS):
            rows = slice(c * chunk, (c + 1) * chunk)
            a = jnp.dot(x[rows], wg_ref[0], preferred_element_type=F32)
            b = jnp.dot(x[rows], wu_ref[0], preferred_element_type=F32)
            hmid = (a * jax.nn.sigmoid(a) * b).astype(BF16)
            part = jnp.dot(hmid, wd_ref[0], preferred_element_type=F32)
            acc_ref[rows, :] = jnp.where(f == 0, part, acc_ref[rows, :] + part)

    @pl.when(f == pl.num_programs(1) - 1)
    def _():
        @pl.when(live)
        def _():
            if dense:
                o_ref[...] = h_ref[...] + _rms(acc_ref[...], g_ref[...])
            else:
                _store_slabs(o_refs, _pack_bf16_pairs(acc_ref[...]))

        @pl.when(jnp.logical_not(live))
        def _():
            for ref in ([o_ref] if dense else o_refs):
                ref[...] = jnp.zeros_like(ref)


def _swiglu(x, wg, wu, wd, block_expert, n_used, n_valid, resid=None, g_post=None):
    dense = resid is not None
    xs = [x] if dense else list(x)
    rows = xs[0].shape[0]
    dff = wg.shape[-1]
    bm = min(FFN_ROWS, rows)
    tf = max(c for c in range(MXU_TILE, min(FFN_COLS_MAX, dff) + 1, MXU_TILE) if dff % c == 0)
    assert rows % bm == 0 and dff % tf == 0
    nf = dff // tf

    def f_eff(j, f, nu):
        return jnp.where(j < nu[0], f, nf - 1)

    row_spec = lambda width: pl.BlockSpec((bm, width), lambda j, f, be, nu, nv: (j, 0))
    in_specs = [row_spec(a.shape[1]) for a in xs] + [
        pl.BlockSpec((1, D_MODEL, tf), lambda j, f, be, nu, nv: (be[j], 0, f_eff(j, f, nu))),
        pl.BlockSpec((1, D_MODEL, tf), lambda j, f, be, nu, nv: (be[j], 0, f_eff(j, f, nu))),
        pl.BlockSpec((1, tf, D_MODEL), lambda j, f, be, nu, nv: (be[j], f_eff(j, f, nu), 0)),
    ]
    args = xs + [wg, wu, wd]
    if dense:
        in_specs += [row_spec(D_MODEL), pl.BlockSpec((1, D_MODEL), lambda j, f, be, nu, nv: (0, 0))]
        args += [resid, g_post.reshape(1, D_MODEL)]
        out_shape, out_specs = jax.ShapeDtypeStruct((rows, D_MODEL), F32), row_spec(D_MODEL)
    else:
        out_shape = [jax.ShapeDtypeStruct((rows, SLAB), jnp.uint32)] * ROW_SPLIT
        out_specs = [row_spec(SLAB)] * ROW_SPLIT
    return pl.pallas_call(
        functools.partial(_swiglu_kernel, dense),
        grid_spec=pltpu.PrefetchScalarGridSpec(
            num_scalar_prefetch=3,
            grid=(rows // bm, nf),
            in_specs=in_specs,
            out_specs=out_specs,
            scratch_shapes=[pltpu.VMEM((bm, D_MODEL), F32)],
        ),
        out_shape=out_shape,
        compiler_params=_params(("parallel", "arbitrary")),
        name="swiglu_dense" if dense else "swiglu_experts",
    )(block_expert, n_used, n_valid, *args)


def _sc_mesh():
    return plsc.VectorSubcoreMesh(core_axis_name="core", subcore_axis_name="subcore")


def _sc_dispatch(slabs, dest, n_rows):
    t, width = slabs[0].shape
    idx = [dest[k].reshape(1, t) for k in range(TOP_K)]
    n_slabs = len(slabs)

    @functools.partial(pl.kernel, mesh=_sc_mesh(), scratch_types=[], name="moe_dispatch_sc",
                       out_type=[jax.ShapeDtypeStruct((n_rows, width), slabs[0].dtype)] * n_slabs)
    def scatter_rows(*refs):
        x_hbm, idx_hbm, o_hbm = refs[:n_slabs], refs[n_slabs:n_slabs + TOP_K], refs[n_slabs + TOP_K:]
        for p in range(n_slabs):
            def body(x_vmem, *idx_vmem, out=o_hbm[p]):
                for i_vmem in idx_vmem:
                    pltpu.sync_copy(x_vmem, out.at[i_vmem.at[0]])

            pltpu.emit_pipeline(
                body,
                grid=(t // SC_WINDOW,),
                in_specs=[pl.BlockSpec((SC_WINDOW, width), lambda i: (i, 0))]
                + [pl.BlockSpec((1, SC_WINDOW), lambda i: (0, i))] * TOP_K,
                out_specs=[],
                core_axis_name=("core", "subcore"),
                dimension_semantics=(pltpu.PARALLEL,),
            )(x_hbm[p], *idx_hbm)

    return scatter_rows(*slabs, *idx)


def _sc_gather(slabs, idx):
    n = idx.shape[0]
    width = slabs[0].shape[1]
    n_slabs = len(slabs)

    @functools.partial(pl.kernel, mesh=_sc_mesh(), scratch_types=[], name="moe_gather_sc",
                       out_type=[jax.ShapeDtypeStruct((n, width), slabs[0].dtype)] * n_slabs)
    def gather_rows(*refs):
        x_hbm, i_hbm, o_hbm = refs[:n_slabs], refs[n_slabs], refs[n_slabs + 1:]
        for p in range(n_slabs):
            def body(i_vmem, o_vmem, src=x_hbm[p]):
                pltpu.sync_copy(src.at[i_vmem.at[0]], o_vmem)

            pltpu.emit_pipeline(
                body,
                grid=(n // SC_WINDOW,),
                in_specs=[pl.BlockSpec((1, SC_WINDOW), lambda i: (0, i))],
                out_specs=[pl.BlockSpec((SC_WINDOW, width), lambda i: (i, 0))],
                core_axis_name=("core", "subcore"),
                dimension_semantics=(pltpu.PARALLEL,),
            )(i_hbm, o_hbm[p])

    return gather_rows(*slabs, idx.reshape(1, n))


def _combine_kernel(*refs):
    y_refs, (route_ref, h_ref, g_ref, o_ref) = refs[:ROW_SPLIT], refs[ROW_SPLIT:]
    per_token = route_ref[...].T
    gate0 = ROUTE_ROWS.index("gate0")
    f = _unpack_bf16_pairs(_load_slabs(y_refs, (0,))) * per_token[:, gate0:gate0 + 1]
    for k in range(1, TOP_K):
        f = f + _unpack_bf16_pairs(_load_slabs(y_refs, (k,))) * per_token[:, gate0 + k:gate0 + k + 1]
    o_ref[...] = h_ref[...] + _rms(f, g_ref[...])


def _combine(y_slabs, record, h, g_post):
    t = h.shape[0]
    tm = min(ROW_TILE, t)
    return pl.pallas_call(
        _combine_kernel,
        grid=(t // tm,),
        in_specs=[pl.BlockSpec((TOP_K, tm, SLAB), lambda i: (0, i, 0))] * ROW_SPLIT + [
            pl.BlockSpec((SUBLANES, tm), lambda i: (0, i)),
            pl.BlockSpec((tm, D_MODEL), lambda i: (i, 0)),
            pl.BlockSpec((1, D_MODEL), lambda i: (0, 0)),
        ],
        out_specs=pl.BlockSpec((tm, D_MODEL), lambda i: (i, 0)),
        out_shape=jax.ShapeDtypeStruct((t, D_MODEL), F32),
        compiler_params=_params(("parallel",)),
        name="moe_combine",
    )(*y_slabs, record, h, g_post.reshape(1, D_MODEL))


def _route(record, counts, bm):
    t = record.shape[1]
    rows = {name: record[pos] for pos, name in enumerate(ROUTE_ROWS)}
    counts = counts[:N_EXPERTS, 0].astype(jnp.int32)
    padded = (counts + bm - 1) // bm * bm
    pad_end = jnp.cumsum(padded)
    pad_start = pad_end - padded
    dest = []
    for k in range(TOP_K):
        choice = rows[f"choice{k}"].astype(jnp.int32)
        start = sum(jnp.where(choice == e, pad_start[e], 0) for e in range(N_EXPERTS))
        dest.append(start + rows[f"rank{k}"].astype(jnp.int32))
    dest = jnp.stack(dest, axis=0)
    n_blocks = (t * TOP_K) // bm + N_EXPERTS
    n_used = (pad_end[-1] // bm).astype(jnp.int32)
    blk = jnp.arange(n_blocks, dtype=jnp.int32)
    last = jnp.minimum(blk, n_used - 1) * bm
    block_expert = jnp.sum((last[:, None] >= pad_end[None, :]).astype(jnp.int32), axis=1)
    block_expert = jnp.minimum(block_expert, N_EXPERTS - 1)
    n_valid = jnp.clip(pad_start[block_expert] + counts[block_expert] - blk * bm, 0, bm)
    n_valid = jnp.where(blk < n_used, n_valid, 0).astype(jnp.int32)
    return dest, block_expert, n_used.reshape(1), n_valid, n_blocks * bm


def kernel(x, mix_norm_pre, mix_norm_post, ffn_norm_pre, ffn_norm_post, w_in, pool_w, pool_scale, na_rpb,
           w_out, dense_w_gate, dense_w_up, dense_w_down, moe_router, moe_w_gate, moe_w_up, moe_w_down):
    b, s, d = x.shape
    assert d == D_MODEL and s % GRID_W == 0
    t = b * s
    rows = s // GRID_W
    depth = w_in.shape[0]
    h = x.reshape(t, d)
    for layer in range(depth):
        j = layer // 2
        is_moe = layer % 2 == 1
        proj = _norm_proj(h, mix_norm_pre[layer], w_in[layer].astype(BF16))
        y_pool = _pool_mixer(proj.reshape(b, s, D_IN), pool_w[layer].astype(BF16), pool_scale[layer])
        y_na = _neighbourhood_attention(proj.reshape(b, rows, GRID_W, D_IN), na_rpb[layer])
        router = None
        if is_moe:
            router = jnp.pad(moe_router[j].T, ((0, EXPERT_ROWS - N_EXPERTS), (0, 0))).astype(BF16)
        outs = _out_proj(y_pool.reshape(t, D_POOL), y_na.reshape(t, D_ATTN), w_out[layer].astype(BF16), h,
                         mix_norm_post[layer], ffn_norm_pre[layer], router)
        if is_moe:
            h1, record, counts = outs[:3]
            dest, block_expert, n_used, n_valid, n_rows = _route(record, counts, min(FFN_ROWS, t))
            xs = _sc_dispatch(list(outs[3:]), dest, n_rows)
            ys = _swiglu(xs, moe_w_gate[j].astype(BF16), moe_w_up[j].astype(BF16),
                         moe_w_down[j].astype(BF16), block_expert, n_used, n_valid)
            by_choice = _sc_gather(list(ys), dest.reshape(TOP_K * t))
            h = _combine([y.reshape(TOP_K, t, SLAB) for y in by_choice], record, h1, ffn_norm_post[layer])
        else:
            h1, hn = outs
            bm = min(FFN_ROWS, t)
            n_blk = t // bm
            h = _swiglu(hn, dense_w_gate[j][None].astype(BF16), dense_w_up[j][None].astype(BF16),
                        dense_w_down[j][None].astype(BF16), jnp.zeros((n_blk,), jnp.int32),
                        jnp.full((1,), n_blk, jnp.int32), jnp.full((n_blk,), bm, jnp.int32),
                        resid=h1, g_post=ffn_norm_post[layer])
    return h.reshape(b, s, d)
```

```python
import functools

import jax
import jax.numpy as jnp
import numpy as np
from jax import lax
from jax.experimental import pallas as pl
from jax.experimental.pallas import tpu as pltpu
from jax.experimental.pallas import tpu_sc as plsc

F32 = jnp.float32
BF16 = jnp.bfloat16

D_MODEL = 1024
D_POOL = 512
POOL_WINDOWS = (2, 4, 8, 16)
POOL_GROUP_DIM = 128
D_ATTN = 512
NA_HEAD_DIM = 32
NA_HEADS = 16
D_IN = D_POOL + 3 * D_ATTN
GRID_W = 64
NA_KH = 8
NA_KW = 16
N_EXPERTS = 8
TOP_K = 2
RMS_EPS = 1e-6
NEG_BIG = -1e30
LOG2E = 1.4426950408889634

LANES = 128
SUBLANES = 8
HEADS_PER_GROUP = 256 // NA_HEAD_DIM
VMEM_LIMIT = 52 * 1024 * 1024

ROW_TILE = 1024
POOL_SUB = 128
POOL_HALO = 64
FFN_ROWS = 512
FFN_ROW_CHUNKS = 2
FFN_COLS_MAX = 2048
MXU_TILE = 256
NA_ROWS_PER_STEP = 4
NA_SLOTS = 8
MOE_COMBINE_CHUNKS = 4
SC_WINDOW = 128
ROW_SPLIT = 2


def _rms(x, g):
    ms = jnp.mean(x * x, axis=-1, keepdims=True)
    return x * lax.rsqrt(ms + RMS_EPS) * g


def _params(sem):
    return pltpu.CompilerParams(dimension_semantics=sem, vmem_limit_bytes=VMEM_LIMIT)


def _norm_proj_kernel(x_ref, g_ref, w_ref, o_ref):
    hn = _rms(x_ref[...], g_ref[...])
    p = jnp.dot(hn.astype(BF16), w_ref[...], preferred_element_type=F32)
    q_lo, q_hi = D_POOL, D_POOL + D_ATTN
    o_ref[:, :q_lo] = p[:, :q_lo].astype(BF16)
    o_ref[:, q_lo:q_hi] = (p[:, q_lo:q_hi] * (NA_HEAD_DIM ** -0.5 * LOG2E)).astype(BF16)
    o_ref[:, q_hi:] = p[:, q_hi:].astype(BF16)


def _norm_proj(h, g, w_bf16):
    t = h.shape[0]
    tm = min(ROW_TILE, t)
    return pl.pallas_call(
        _norm_proj_kernel,
        grid=(t // tm,),
        in_specs=[
            pl.BlockSpec((tm, D_MODEL), lambda i: (i, 0)),
            pl.BlockSpec((1, D_MODEL), lambda i: (0, 0)),
            pl.BlockSpec((D_MODEL, D_IN), lambda i: (0, 0)),
        ],
        out_specs=pl.BlockSpec((tm, D_IN), lambda i: (i, 0)),
        out_shape=jax.ShapeDtypeStruct((t, D_IN), BF16),
        compiler_params=_params(("parallel",)),
        name="norm_proj",
    )(h, g.reshape(1, D_MODEL), w_bf16)


def _pool_bands():
    kdim = POOL_SUB + 2 * POOL_HALO
    rel = np.arange(kdim)[None, :] - np.arange(POOL_SUB)[:, None] - POOL_HALO
    return np.stack([((rel >= -(w // 2)) & (rel < w - w // 2)) for w in POOL_WINDOWS]).astype(np.float32)


def _pool_kernel(seq_len, ts, cur_ref, prev_ref, next_ref, band_ref, pw_ref, ps_ref, o_ref, win_ref):
    i = pl.program_id(1)
    t0 = i * ts
    halo_zero = jnp.zeros((POOL_HALO, D_POOL), BF16)
    win_ref[:POOL_HALO, :] = jnp.where(i == 0, halo_zero, prev_ref[0])
    win_ref[POOL_HALO:POOL_HALO + ts, :] = cur_ref[0]
    win_ref[POOL_HALO + ts:, :] = jnp.where(i == pl.num_programs(1) - 1, halo_zero, next_ref[0])
    kdim = POOL_SUB + 2 * POOL_HALO
    t_abs = t0 + lax.broadcasted_iota(jnp.int32, (ts, 1), 0)
    for g, w in enumerate(POOL_WINDOWS):
        half = w // 2
        c0, c1 = g * POOL_GROUP_DIM, (g + 1) * POOL_GROUP_DIM
        cnt = (jnp.minimum(t_abs + (w - half), seq_len) - jnp.maximum(t_abs - half, 0)).astype(F32)
        wsum = jnp.concatenate(
            [jnp.dot(band_ref[g], win_ref[s * POOL_SUB:s * POOL_SUB + kdim, c0:c1], preferred_element_type=F32)
             for s in range(ts // POOL_SUB)], axis=0)
        delta = wsum / cnt - cur_ref[0, :, c0:c1].astype(F32)
        y = jnp.dot(delta.astype(BF16), pw_ref[g], preferred_element_type=F32)
        o_ref[0, :, c0:c1] = (y * ps_ref[:, c0:c1]).astype(BF16)


def _pool_mixer(proj3, pool_w_bf16, pool_scale):
    b, s, _ = proj3.shape
    ts = min(ROW_TILE, s)
    hb = ts // POOL_HALO
    n_halo = s // POOL_HALO
    bands = jnp.asarray(_pool_bands(), BF16)
    return pl.pallas_call(
        functools.partial(_pool_kernel, s, ts),
        grid=(b, s // ts),
        in_specs=[
            pl.BlockSpec((1, ts, D_POOL), lambda bi, i: (bi, i, 0)),
            pl.BlockSpec((1, POOL_HALO, D_POOL), lambda bi, i: (bi, jnp.maximum(i * hb - 1, 0), 0)),
            pl.BlockSpec((1, POOL_HALO, D_POOL),
                         lambda bi, i: (bi, jnp.minimum((i + 1) * hb, n_halo - 1), 0)),
            pl.BlockSpec(bands.shape, lambda bi, i: (0, 0, 0)),
            pl.BlockSpec((len(POOL_WINDOWS), POOL_GROUP_DIM, POOL_GROUP_DIM), lambda bi, i: (0, 0, 0)),
            pl.BlockSpec((1, D_POOL), lambda bi, i: (0, 0)),
        ],
        out_specs=pl.BlockSpec((1, ts, D_POOL), lambda bi, i: (bi, i, 0)),
        out_shape=jax.ShapeDtypeStruct((b, s, D_POOL), BF16),
        scratch_shapes=[pltpu.VMEM((ts + 2 * POOL_HALO, D_POOL), BF16)],
        compiler_params=_params(("parallel", "parallel")),
        name="pool_mixer",
    )(proj3, proj3, proj3, bands, pool_w_bf16, pool_scale.reshape(1, D_POOL))


RPB_SHIFT = GRID_W - NA_KW


def _na_build_bias(rpb_ref, bias_ref):
    n_dr = rpb_ref.shape[1]
    c = lax.broadcasted_iota(jnp.int32, (GRID_W, 2 * GRID_W), 0)
    lane = lax.broadcasted_iota(jnp.int32, (GRID_W, 2 * GRID_W), 1)
    kc = lane % GRID_W
    c0 = jnp.clip(c - NA_KW // 2, 0, GRID_W - NA_KW)
    inside = (kc >= c0) & (kc < c0 + NA_KW)
    first_half = lane < GRID_W

    def per_head(h, carry):
        halves = []
        for dr in range(n_dr):
            row = jnp.broadcast_to(rpb_ref[h, dr:dr + 1, :], (GRID_W, 2 * GRID_W)) * LOG2E
            halves.append((pltpu.roll(row, GRID_W + 1, 1, stride=1, stride_axis=0),
                           pltpu.roll(row, 1, 1, stride=1, stride_axis=0)))
        for e in range(n_dr - 1):
            pair = jnp.where(first_half, halves[e][0], halves[e + 1][1])
            bias_ref[h, e] = jnp.where(inside, pair, NEG_BIG)
        return carry

    lax.fori_loop(0, rpb_ref.shape[0], per_head, 0)


def _na_kernel(n_rows, q_ref, k_ref, v_ref, rpb_ref, o_ref, s_ref, p_ref, bias_ref):
    @pl.when(pl.program_id(0) == 0)
    def _():
        _na_build_bias(rpb_ref, bias_ref)

    n_batch, rows_per_step = q_ref.shape[0], q_ref.shape[1]
    win_rows = k_ref.shape[1]
    gw = HEADS_PER_GROUP * NA_HEAD_DIM
    n_groups = NA_HEADS // HEADS_PER_GROUP
    n_slots = s_ref.shape[0]
    lane_head = lax.broadcasted_iota(jnp.int32, (HEADS_PER_GROUP, 1, gw), 2) // NA_HEAD_DIM
    head_id = lax.broadcasted_iota(jnp.int32, (HEADS_PER_GROUP, 1, gw), 0)
    own = lane_head == head_id
    out_head = lax.broadcasted_iota(jnp.int32, (GRID_W, gw), 1) // NA_HEAD_DIM
    first_row = pl.program_id(0) * rows_per_step
    win0 = jnp.clip(first_row - NA_KH // 2, 0, n_rows - win_rows)
    chain = 0
    for j in range(rows_per_step):
        r = first_row + j
        r0 = jnp.clip(r - NA_KH // 2, 0, n_rows - NA_KH)
        off = r0 - win0
        d = r0 - r + NA_KH - 1
        for bi in range(n_batch):
            for g in range(n_groups):
                slot = chain % n_slots
                chain += 1
                lo, hi = g * gw, (g + 1) * gw
                qg = q_ref[bi, j, :, lo:hi]
                zero = jnp.zeros_like(qg)
                qm = jnp.where(own, qg[None], zero[None]).reshape(HEADS_PER_GROUP * GRID_W, gw)
                kw = k_ref[bi, pl.ds(off, NA_KH), :, lo:hi].reshape(NA_KH * GRID_W, gw)
                vw = v_ref[bi, pl.ds(off, NA_KH), :, lo:hi].reshape(NA_KH * GRID_W, gw)
                s_ref[slot] = lax.dot_general(qm, kw, (((1,), (1,)), ((), ())), preferred_element_type=F32)
                inv_l = []
                for h in range(HEADS_PER_GROUP):
                    rows = slice(h * GRID_W, (h + 1) * GRID_W)
                    bias = jnp.concatenate([bias_ref[g * HEADS_PER_GROUP + h, d + 2 * pair]
                                            for pair in range(NA_KH // 2)], axis=1)
                    s = s_ref[slot, rows, :] + bias
                    p = jnp.exp2(s - jnp.max(s, axis=-1, keepdims=True))
                    inv_l.append(1.0 / jnp.sum(p, axis=-1, keepdims=True))
                    p_ref[slot, rows, :] = p.astype(BF16)
                pv = jnp.dot(p_ref[slot], vw, preferred_element_type=F32)
                out = pv[:GRID_W] * inv_l[0]
                for h in range(1, HEADS_PER_GROUP):
                    out = jnp.where(out_head == h, pv[h * GRID_W:(h + 1) * GRID_W] * inv_l[h], out)
                o_ref[bi, j, :, lo:hi] = out.astype(BF16)


def _neighbourhood_attention(proj4, rpb):
    b, rows, w, _ = proj4.shape
    assert w == GRID_W and rows >= NA_KH
    n_heads, n_dr, n_dc = rpb.shape
    rpb_rows = jnp.pad(rpb.astype(F32), ((0, 0), (0, 0), (RPB_SHIFT, 2 * GRID_W - RPB_SHIFT - n_dc)))
    rps = next(c for c in (NA_ROWS_PER_STEP, 2, 1) if rows % c == 0 and rows >= c + NA_KH - 1)
    win_rows = rps + NA_KH - 1
    n_stack = HEADS_PER_GROUP * GRID_W
    n_keys = NA_KH * GRID_W

    def window(col_block):
        shape = (pl.Element(b), pl.Element(win_rows), pl.Element(GRID_W), pl.Element(D_ATTN))
        return pl.BlockSpec(
            shape, lambda i: (0, jnp.clip(i * rps - NA_KH // 2, 0, rows - win_rows), 0, col_block * D_ATTN))

    return pl.pallas_call(
        functools.partial(_na_kernel, rows),
        grid=(rows // rps,),
        in_specs=[pl.BlockSpec((b, rps, GRID_W, D_ATTN), lambda i: (0, i, 0, 1)), window(2), window(3),
                  pl.BlockSpec(rpb_rows.shape, lambda i: (0, 0, 0))],
        out_specs=pl.BlockSpec((b, rps, GRID_W, D_ATTN), lambda i: (0, i, 0, 0)),
        out_shape=jax.ShapeDtypeStruct((b, rows, GRID_W, D_ATTN), BF16),
        scratch_shapes=[pltpu.VMEM((NA_SLOTS, n_stack, n_keys), F32),
                        pltpu.VMEM((NA_SLOTS, n_stack, n_keys), BF16),
                        pltpu.VMEM((n_heads, n_dr - 1, GRID_W, 2 * GRID_W), F32)],
        compiler_params=_params(("arbitrary",)),
        name="neighbourhood_attention",
    )(proj4, proj4, proj4, rpb_rows)


def _pack_bf16_pairs(x):
    n = x.shape[1] // 2
    hi = lax.bitcast_convert_type(x[:, :n].astype(BF16).astype(F32), jnp.uint32)
    lo = lax.bitcast_convert_type(x[:, n:].astype(BF16).astype(F32), jnp.uint32)
    return hi | (lo >> 16)


def _unpack_bf16_pairs(w):
    hi = lax.bitcast_convert_type(w & jnp.uint32(0xFFFF0000), F32)
    lo = lax.bitcast_convert_type(w << 16, F32)
    return jnp.concatenate([hi, lo], axis=1)


SLAB = D_MODEL // 2 // ROW_SPLIT


def _store_slabs(slab_refs, words):
    for p, ref in enumerate(slab_refs):
        ref[...] = words[:, p * SLAB:(p + 1) * SLAB]


def _load_slabs(slab_refs, lead=()):
    return jnp.concatenate([ref[lead + (slice(None), slice(None))] for ref in slab_refs], axis=1)


ROUTE_ROWS = ("choice0", "choice1", "gate0", "gate1", "rank0", "rank1")
EXPERT_ROWS = 16


def _top2_route(logits_t, earlier_ref, carry_ref):
    eid = lax.broadcasted_iota(jnp.int32, logits_t.shape, 0).astype(F32)
    valid = jnp.where(eid < N_EXPERTS, logits_t, -jnp.inf)
    top = []
    for _ in range(TOP_K):
        m = jnp.max(valid, axis=0, keepdims=True)
        idx = jnp.min(jnp.where(valid == m, eid, float(EXPERT_ROWS)), axis=0, keepdims=True)
        top.append((m, idx))
        valid = jnp.where(eid == idx, -jnp.inf, valid)
    (m0, i0), (m1, i1) = top
    e = jnp.exp(m1 - m0)
    g0 = 1.0 / (1.0 + e)
    g1 = e * g0
    hot0 = (eid == i0).astype(F32)
    hot1 = (eid == i1).astype(F32)
    picked = hot0 + hot1
    carry = carry_ref[:, 0:1]
    earlier = jnp.dot(picked.astype(BF16), earlier_ref[...], preferred_element_type=F32) + carry
    r0 = jnp.sum(earlier * hot0, axis=0, keepdims=True)
    r1 = jnp.sum(earlier * hot1, axis=0, keepdims=True)
    row = lax.broadcasted_iota(jnp.int32, (SUBLANES, logits_t.shape[1]), 0)
    record = jnp.zeros((SUBLANES, logits_t.shape[1]), F32)
    for pos, val in enumerate((i0, i1, g0, g1, r0, r1)):
        record = jnp.where(row == pos, val, record)
    return record, carry + jnp.sum(picked, axis=1, keepdims=True)


def _out_proj_kernel(with_router, yp_ref, ya_ref, w_ref, h_ref, gpost_ref, gpre_ref, *refs):
    mix = jnp.dot(yp_ref[...], w_ref[:D_POOL, :], preferred_element_type=F32)
    mix = mix + jnp.dot(ya_ref[...], w_ref[D_POOL:, :], preferred_element_type=F32)
    h1 = h_ref[...] + _rms(mix, gpost_ref[...])
    if with_router:
        hn = _rms(h1, gpre_ref[...])
        wr_ref, earlier_ref, h1_ref, route_ref, counts_ref = refs[:5]
        slab_refs, carry_ref = refs[5:-1], refs[-1]
        _store_slabs(slab_refs, _pack_bf16_pairs(hn))

        @pl.when(pl.program_id(0) == 0)
        def _():
            carry_ref[...] = jnp.zeros_like(carry_ref)

        logits_t = lax.dot_general(wr_ref[...], hn.astype(BF16), (((1,), (1,)), ((), ())),
                                   preferred_element_type=F32)
        record, counts = _top2_route(logits_t, earlier_ref, carry_ref)
        route_ref[...] = record
        carry_ref[...] = jnp.broadcast_to(counts, carry_ref.shape)
        counts_ref[...] = jnp.broadcast_to(counts, counts_ref.shape)
    else:
        (h1_ref,) = refs
    h1_ref[...] = h1


def _out_proj(y_pool, y_na, w_out_bf16, h, g_post, g_pre, router_bf16=None):
    t = h.shape[0]
    tm = min(ROW_TILE, t)
    with_router = router_bf16 is not None
    row = lambda width: pl.BlockSpec((tm, width), lambda i: (i, 0))
    full = lambda a, b: pl.BlockSpec((a, b), lambda i: (0, 0))
    in_specs = [row(D_POOL), row(D_ATTN), full(D_MODEL, D_MODEL), row(D_MODEL),
                full(1, D_MODEL), full(1, D_MODEL)]
    args = [y_pool, y_na, w_out_bf16, h, g_post.reshape(1, D_MODEL), g_pre.reshape(1, D_MODEL)]
    scratch = []
    if with_router:
        earlier_tokens = jnp.asarray(np.triu(np.ones((tm, tm), np.float32), 1), BF16)
        in_specs += [full(EXPERT_ROWS, D_MODEL), full(tm, tm)]
        args += [router_bf16, earlier_tokens]
        out_specs = [row(D_MODEL), pl.BlockSpec((SUBLANES, tm), lambda i: (0, i)), full(EXPERT_ROWS, LANES)]
        out_specs += [row(SLAB)] * ROW_SPLIT
        out_shape = [jax.ShapeDtypeStruct((t, D_MODEL), F32), jax.ShapeDtypeStruct((SUBLANES, t), F32),
                     jax.ShapeDtypeStruct((EXPERT_ROWS, LANES), F32)]
        out_shape += [jax.ShapeDtypeStruct((t, SLAB), jnp.uint32)] * ROW_SPLIT
        scratch = [pltpu.VMEM((EXPERT_ROWS, LANES), F32)]
    else:
        out_specs = [row(D_MODEL)]
        out_shape = [jax.ShapeDtypeStruct((t, D_MODEL), F32)]
    return pl.pallas_call(
        functools.partial(_out_proj_kernel, with_router),
        grid=(t // tm,),
        in_specs=in_specs,
        out_specs=out_specs,
        out_shape=out_shape,
        scratch_shapes=scratch,
        compiler_params=_params(("arbitrary",) if with_router else ("parallel",)),
        name="out_proj_router" if with_router else "out_proj",
    )(*args)


def _swiglu_kernel(dense, be_ref, nused_ref, nvalid_ref, *refs):
    if dense:
        wg_ref, wu_ref, wd_ref, h_ref, gpre_ref, g_ref, o_ref, acc_ref = refs
    else:
        x_refs, (wg_ref, wu_ref, wd_ref) = refs[:ROW_SPLIT], refs[ROW_SPLIT:ROW_SPLIT + 3]
        o_refs, acc_ref = refs[ROW_SPLIT + 3:-1], refs[-1]
    j = pl.program_id(0)
    f = pl.program_id(1)
    live = j < nused_ref[0]

    @pl.when((j == 0) & (f == 0))
    def _():
        acc_ref[...] = jnp.zeros_like(acc_ref)

    @pl.when(live)
    def _():
        if dense:
            x = _rms(h_ref[...], gpre_ref[...]).astype(BF16)
        else:
            row = lax.broadcasted_iota(jnp.int32, (acc_ref.shape[0], 1), 0)
            words = jnp.where(row < nvalid_ref[j], _load_slabs(x_refs), jnp.uint32(0))
            x = _unpack_bf16_pairs(words).astype(BF16)
        chunk = x.shape[0] // FFN_ROW_CHUNKS
        for c in range(FFN_ROW_CHUNKS):
            rows = slice(c * chunk, (c + 1) * chunk)
            a = jnp.dot(x[rows], wg_ref[0], preferred_element_type=F32)
            b = jnp.dot(x[rows], wu_ref[0], preferred_element_type=F32)
            hmid = (a * jax.nn.sigmoid(a) * b).astype(BF16)
            part = jnp.dot(hmid, wd_ref[0], preferred_element_type=F32)
            acc_ref[rows, :] = jnp.where(f == 0, part, acc_ref[rows, :] + part)

    @pl.when(f == pl.num_programs(1) - 1)
    def _():
        @pl.when(live)
        def _():
            if dense:
                o_ref[...] = h_ref[...] + _rms(acc_ref[...], g_ref[...])
            else:
                _store_slabs(o_refs, _pack_bf16_pairs(acc_ref[...]))

        @pl.when(jnp.logical_not(live))
        def _():
            for ref in ([o_ref] if dense else o_refs):
                ref[...] = jnp.zeros_like(ref)


def _swiglu(x, wg, wu, wd, block_expert, n_used, n_valid, g_pre=None, g_post=None):
    dense = g_pre is not None
    xs = [] if dense else list(x)
    rows = x.shape[0] if dense else xs[0].shape[0]
    dff = wg.shape[-1]
    bm = min(FFN_ROWS, rows)
    tf = max(c for c in range(MXU_TILE, min(FFN_COLS_MAX, dff) + 1, MXU_TILE) if dff % c == 0)
    assert rows % bm == 0 and dff % tf == 0
    nf = dff // tf

    def f_eff(j, f, nu):
        return jnp.where(j < nu[0], f, nf - 1)

    row_spec = lambda width: pl.BlockSpec((bm, width), lambda j, f, be, nu, nv: (j, 0))
    in_specs = [row_spec(a.shape[1]) for a in xs] + [
        pl.BlockSpec((1, D_MODEL, tf), lambda j, f, be, nu, nv: (be[j], 0, f_eff(j, f, nu))),
        pl.BlockSpec((1, D_MODEL, tf), lambda j, f, be, nu, nv: (be[j], 0, f_eff(j, f, nu))),
        pl.BlockSpec((1, tf, D_MODEL), lambda j, f, be, nu, nv: (be[j], f_eff(j, f, nu), 0)),
    ]
    args = xs + [wg, wu, wd]
    if dense:
        gain_spec = pl.BlockSpec((1, D_MODEL), lambda j, f, be, nu, nv: (0, 0))
        in_specs += [row_spec(D_MODEL), gain_spec, gain_spec]
        args += [x, g_pre.reshape(1, D_MODEL), g_post.reshape(1, D_MODEL)]
        out_shape, out_specs = jax.ShapeDtypeStruct((rows, D_MODEL), F32), row_spec(D_MODEL)
    else:
        out_shape = [jax.ShapeDtypeStruct((rows, SLAB), jnp.uint32)] * ROW_SPLIT
        out_specs = [row_spec(SLAB)] * ROW_SPLIT
    return pl.pallas_call(
        functools.partial(_swiglu_kernel, dense),
        grid_spec=pltpu.PrefetchScalarGridSpec(
            num_scalar_prefetch=3,
            grid=(rows // bm, nf),
            in_specs=in_specs,
            out_specs=out_specs,
            scratch_shapes=[pltpu.VMEM((bm, D_MODEL), F32)],
        ),
        out_shape=out_shape,
        compiler_params=_params(("parallel", "arbitrary")),
        name="swiglu_dense" if dense else "swiglu_experts",
    )(block_expert, n_used, n_valid, *args)


def _sc_mesh():
    return plsc.VectorSubcoreMesh(core_axis_name="core", subcore_axis_name="subcore")


def _sc_dispatch(slabs, dest, n_rows):
    t, width = slabs[0].shape
    idx = [dest[k].reshape(1, t) for k in range(TOP_K)]
    n_slabs = len(slabs)

    @functools.partial(pl.kernel, mesh=_sc_mesh(), scratch_types=[], name="moe_dispatch_sc",
                       out_type=[jax.ShapeDtypeStruct((n_rows, width), slabs[0].dtype)] * n_slabs)
    def scatter_rows(*refs):
        x_hbm, idx_hbm, o_hbm = refs[:n_slabs], refs[n_slabs:n_slabs + TOP_K], refs[n_slabs + TOP_K:]
        for p in range(n_slabs):
            def body(x_vmem, *idx_vmem, out=o_hbm[p]):
                for i_vmem in idx_vmem:
                    pltpu.sync_copy(x_vmem, out.at[i_vmem.at[0]])

            pltpu.emit_pipeline(
                body,
                grid=(t // SC_WINDOW,),
                in_specs=[pl.BlockSpec((SC_WINDOW, width), lambda i: (i, 0))]
                + [pl.BlockSpec((1, SC_WINDOW), lambda i: (0, i))] * TOP_K,
                out_specs=[],
                core_axis_name=("core", "subcore"),
                dimension_semantics=(pltpu.PARALLEL,),
            )(x_hbm[p], *idx_hbm)

    return scatter_rows(*slabs, *idx)


def _sc_gather(slabs, idx):
    n = idx.shape[0]
    width = slabs[0].shape[1]
    n_slabs = len(slabs)

    @functools.partial(pl.kernel, mesh=_sc_mesh(), scratch_types=[], name="moe_gather_sc",
                       out_type=[jax.ShapeDtypeStruct((n, width), slabs[0].dtype)] * n_slabs)
    def gather_rows(*refs):
        x_hbm, i_hbm, o_hbm = refs[:n_slabs], refs[n_slabs], refs[n_slabs + 1:]
        for p in range(n_slabs):
            def body(i_vmem, o_vmem, src=x_hbm[p]):
                pltpu.sync_copy(src.at[i_vmem.at[0]], o_vmem)

            pltpu.emit_pipeline(
                body,
                grid=(n // SC_WINDOW,),
                in_specs=[pl.BlockSpec((1, SC_WINDOW), lambda i: (0, i))],
                out_specs=[pl.BlockSpec((SC_WINDOW, width), lambda i: (i, 0))],
                core_axis_name=("core", "subcore"),
                dimension_semantics=(pltpu.PARALLEL,),
            )(i_hbm, o_hbm[p])

    return gather_rows(*slabs, idx.reshape(1, n))


def _combine_kernel(*refs):
    y_refs, (route_ref, h_ref, g_ref, o_ref) = refs[:ROW_SPLIT], refs[ROW_SPLIT:]
    per_token = route_ref[...].T
    gate0 = ROUTE_ROWS.index("gate0")
    f = _unpack_bf16_pairs(_load_slabs(y_refs, (0,))) * per_token[:, gate0:gate0 + 1]
    for k in range(1, TOP_K):
        f = f + _unpack_bf16_pairs(_load_slabs(y_refs, (k,))) * per_token[:, gate0 + k:gate0 + k + 1]
    o_ref[...] = h_ref[...] + _rms(f, g_ref[...])


def _combine(y_slabs, record, h, g_post, first_token):
    tc = y_slabs[0].shape[1]
    tm = min(ROW_TILE, tc)
    assert tc % tm == 0 and first_token % tm == 0
    tile0 = first_token // tm
    return pl.pallas_call(
        _combine_kernel,
        grid=(tc // tm,),
        in_specs=[pl.BlockSpec((TOP_K, tm, SLAB), lambda i: (0, i, 0))] * ROW_SPLIT + [
            pl.BlockSpec((SUBLANES, tm), lambda i: (0, tile0 + i)),
            pl.BlockSpec((tm, D_MODEL), lambda i: (tile0 + i, 0)),
            pl.BlockSpec((1, D_MODEL), lambda i: (0, 0)),
        ],
        out_specs=pl.BlockSpec((tm, D_MODEL), lambda i: (tile0 + i, 0)),
        out_shape=jax.ShapeDtypeStruct(h.shape, F32),
        input_output_aliases={ROW_SPLIT + 1: 0},
        compiler_params=_params(("parallel",)),
        name="moe_combine",
    )(*y_slabs, record, h, g_post.reshape(1, D_MODEL))


def _route(record, counts, bm):
    t = record.shape[1]
    rows = {name: record[pos] for pos, name in enumerate(ROUTE_ROWS)}
    counts = counts[:N_EXPERTS, 0].astype(jnp.int32)
    padded = (counts + bm - 1) // bm * bm
    pad_end = jnp.cumsum(padded)
    pad_start = pad_end - padded
    dest = []
    for k in range(TOP_K):
        choice = rows[f"choice{k}"].astype(jnp.int32)
        start = sum(jnp.where(choice == e, pad_start[e], 0) for e in range(N_EXPERTS))
        dest.append(start + rows[f"rank{k}"].astype(jnp.int32))
    dest = jnp.stack(dest, axis=0)
    n_blocks = (t * TOP_K) // bm + N_EXPERTS
    n_used = (pad_end[-1] // bm).astype(jnp.int32)
    blk = jnp.arange(n_blocks, dtype=jnp.int32)
    last = jnp.minimum(blk, n_used - 1) * bm
    block_expert = jnp.sum((last[:, None] >= pad_end[None, :]).astype(jnp.int32), axis=1)
    block_expert = jnp.minimum(block_expert, N_EXPERTS - 1)
    n_valid = jnp.clip(pad_start[block_expert] + counts[block_expert] - blk * bm, 0, bm)
    n_valid = jnp.where(blk < n_used, n_valid, 0).astype(jnp.int32)
    return dest, block_expert, n_used.reshape(1), n_valid, n_blocks * bm


def kernel(x, mix_norm_pre, mix_norm_post, ffn_norm_pre, ffn_norm_post, w_in, pool_w, pool_scale, na_rpb,
           w_out, dense_w_gate, dense_w_up, dense_w_down, moe_router, moe_w_gate, moe_w_up, moe_w_down):
    b, s, d = x.shape
    assert d == D_MODEL and s % GRID_W == 0
    t = b * s
    rows = s // GRID_W
    depth = w_in.shape[0]
    h = x.reshape(t, d)
    for layer in range(depth):
        j = layer // 2
        is_moe = layer % 2 == 1
        proj = _norm_proj(h, mix_norm_pre[layer], w_in[layer].astype(BF16))
        y_pool = _pool_mixer(proj.reshape(b, s, D_IN), pool_w[layer].astype(BF16), pool_scale[layer])
        y_na = _neighbourhood_attention(proj.reshape(b, rows, GRID_W, D_IN), na_rpb[layer])
        router = None
        if is_moe:
            router = jnp.pad(moe_router[j].T, ((0, EXPERT_ROWS - N_EXPERTS), (0, 0))).astype(BF16)
        outs = _out_proj(y_pool.reshape(t, D_POOL), y_na.reshape(t, D_ATTN), w_out[layer].astype(BF16), h,
                         mix_norm_post[layer], ffn_norm_pre[layer], router)
        if is_moe:
            h1, record, counts = outs[:3]
            dest, block_expert, n_used, n_valid, n_rows = _route(record, counts, min(FFN_ROWS, t))
            xs = _sc_dispatch(list(outs[3:]), dest, n_rows)
            ys = _swiglu(xs, moe_w_gate[j].astype(BF16), moe_w_up[j].astype(BF16),
                         moe_w_down[j].astype(BF16), block_expert, n_used, n_valid)
            tc = t // MOE_COMBINE_CHUNKS
            h = h1
            for c in range(MOE_COMBINE_CHUNKS):
                idx = dest[:, c * tc:(c + 1) * tc].reshape(TOP_K * tc)
                by_choice = _sc_gather(list(ys), idx)
                h = _combine([y.reshape(TOP_K, tc, SLAB) for y in by_choice], record, h,
                             ffn_norm_post[layer], c * tc)
        else:
            (h1,) = outs
            bm = min(FFN_ROWS, t)
            n_blk = t // bm
            h = _swiglu(h1, dense_w_gate[j][None].astype(BF16), dense_w_up[j][None].astype(BF16),
                        dense_w_down[j][None].astype(BF16), jnp.zeros((n_blk,), jnp.int32),
                        jnp.full((1,), n_blk, jnp.int32), jnp.full((n_blk,), bm, jnp.int32),
                        g_pre=ffn_norm_pre[layer], g_post=ffn_norm_post[layer])
    return h.reshape(b, s, d)
```

```python
import functools

import jax
import jax.numpy as jnp
import numpy as np
from jax import lax
from jax.experimental import pallas as pl
from jax.experimental.pallas import tpu as pltpu
from jax.experimental.pallas import tpu_sc as plsc

F32 = jnp.float32
BF16 = jnp.bfloat16

D_MODEL = 1024
D_POOL = 512
POOL_WINDOWS = (2, 4, 8, 16)
POOL_GROUP_DIM = 128
D_ATTN = 512
NA_HEAD_DIM = 32
NA_HEADS = 16
D_IN = D_POOL + 3 * D_ATTN
GRID_W = 64
NA_KH = 8
NA_KW = 16
N_EXPERTS = 8
TOP_K = 2
RMS_EPS = 1e-6
NEG_BIG = -1e30
LOG2E = 1.4426950408889634

LANES = 128
SUBLANES = 8
HEADS_PER_GROUP = 256 // NA_HEAD_DIM
VMEM_LIMIT = 52 * 1024 * 1024

ROW_TILE = 1024
POOL_SUB = 128
POOL_HALO = 64
FFN_ROWS = 512
FFN_ROW_CHUNKS = 2
FFN_COLS_MAX = 2048
MXU_TILE = 256
NA_ROWS_PER_STEP = 4
NA_SLOTS = 8
SC_WINDOW = 128
ROW_SPLIT = 2


def _rms(x, g):
    ms = jnp.mean(x * x, axis=-1, keepdims=True)
    return x * lax.rsqrt(ms + RMS_EPS) * g


def _params(sem):
    return pltpu.CompilerParams(dimension_semantics=sem, vmem_limit_bytes=VMEM_LIMIT)


def _norm_proj_kernel(x_ref, g_ref, w_ref, o_ref):
    hn = _rms(x_ref[...], g_ref[...])
    p = jnp.dot(hn.astype(BF16), w_ref[...], preferred_element_type=F32)
    q_lo, q_hi = D_POOL, D_POOL + D_ATTN
    o_ref[:, :q_lo] = p[:, :q_lo].astype(BF16)
    o_ref[:, q_lo:q_hi] = (p[:, q_lo:q_hi] * (NA_HEAD_DIM ** -0.5 * LOG2E)).astype(BF16)
    o_ref[:, q_hi:] = p[:, q_hi:].astype(BF16)


def _norm_proj(h, g, w_bf16):
    t = h.shape[0]
    tm = min(ROW_TILE, t)
    return pl.pallas_call(
        _norm_proj_kernel,
        grid=(t // tm,),
        in_specs=[
            pl.BlockSpec((tm, D_MODEL), lambda i: (i, 0)),
            pl.BlockSpec((1, D_MODEL), lambda i: (0, 0)),
            pl.BlockSpec((D_MODEL, D_IN), lambda i: (0, 0)),
        ],
        out_specs=pl.BlockSpec((tm, D_IN), lambda i: (i, 0)),
        out_shape=jax.ShapeDtypeStruct((t, D_IN), BF16),
        compiler_params=_params(("parallel",)),
        name="norm_proj",
    )(h, g.reshape(1, D_MODEL), w_bf16)


def _pool_bands():
    kdim = POOL_SUB + 2 * POOL_HALO
    rel = np.arange(kdim)[None, :] - np.arange(POOL_SUB)[:, None] - POOL_HALO
    return np.stack([((rel >= -(w // 2)) & (rel < w - w // 2)) for w in POOL_WINDOWS]).astype(np.float32)


def _pool_tile(seq_len, cur_ref, prev_ref, next_ref, band_ref, pw_ref, ps_ref, win_ref):
    tm = cur_ref.shape[0]
    tiles_per_seq = seq_len // tm
    i = pl.program_id(0) % tiles_per_seq
    t0 = i * tm
    halo_zero = jnp.zeros((POOL_HALO, D_POOL), BF16)
    win_ref[:POOL_HALO, :] = jnp.where(i == 0, halo_zero, prev_ref[...])
    win_ref[POOL_HALO:POOL_HALO + tm, :] = cur_ref[...]
    win_ref[POOL_HALO + tm:, :] = jnp.where(i == tiles_per_seq - 1, halo_zero, next_ref[...])
    kdim = POOL_SUB + 2 * POOL_HALO
    t_abs = t0 + lax.broadcasted_iota(jnp.int32, (tm, 1), 0)
    groups = []
    for g, w in enumerate(POOL_WINDOWS):
        half = w // 2
        c0, c1 = g * POOL_GROUP_DIM, (g + 1) * POOL_GROUP_DIM
        cnt = (jnp.minimum(t_abs + (w - half), seq_len) - jnp.maximum(t_abs - half, 0)).astype(F32)
        wsum = jnp.concatenate(
            [jnp.dot(band_ref[g], win_ref[s * POOL_SUB:s * POOL_SUB + kdim, c0:c1], preferred_element_type=F32)
             for s in range(tm // POOL_SUB)], axis=0)
        delta = wsum / cnt - cur_ref[:, c0:c1].astype(F32)
        y = jnp.dot(delta.astype(BF16), pw_ref[g], preferred_element_type=F32)
        groups.append((y * ps_ref[:, c0:c1]).astype(BF16))
    return jnp.concatenate(groups, axis=1)


RPB_SHIFT = GRID_W - NA_KW


def _na_build_bias(rpb_ref, bias_ref):
    n_dr = rpb_ref.shape[1]
    c = lax.broadcasted_iota(jnp.int32, (GRID_W, 2 * GRID_W), 0)
    lane = lax.broadcasted_iota(jnp.int32, (GRID_W, 2 * GRID_W), 1)
    kc = lane % GRID_W
    c0 = jnp.clip(c - NA_KW // 2, 0, GRID_W - NA_KW)
    inside = (kc >= c0) & (kc < c0 + NA_KW)
    first_half = lane < GRID_W

    def per_head(h, carry):
        halves = []
        for dr in range(n_dr):
            row = jnp.broadcast_to(rpb_ref[h, dr:dr + 1, :], (GRID_W, 2 * GRID_W)) * LOG2E
            halves.append((pltpu.roll(row, GRID_W + 1, 1, stride=1, stride_axis=0),
                           pltpu.roll(row, 1, 1, stride=1, stride_axis=0)))
        for e in range(n_dr - 1):
            pair = jnp.where(first_half, halves[e][0], halves[e + 1][1])
            bias_ref[h, e] = jnp.where(inside, pair, NEG_BIG)
        return carry

    lax.fori_loop(0, rpb_ref.shape[0], per_head, 0)


def _na_kernel(n_rows, q_ref, k_ref, v_ref, rpb_ref, o_ref, s_ref, p_ref, bias_ref):
    @pl.when(pl.program_id(0) == 0)
    def _():
        _na_build_bias(rpb_ref, bias_ref)

    n_batch, rows_per_step = q_ref.shape[0], q_ref.shape[1]
    win_rows = k_ref.shape[1]
    gw = HEADS_PER_GROUP * NA_HEAD_DIM
    n_groups = NA_HEADS // HEADS_PER_GROUP
    n_slots = s_ref.shape[0]
    lane_head = lax.broadcasted_iota(jnp.int32, (HEADS_PER_GROUP, 1, gw), 2) // NA_HEAD_DIM
    head_id = lax.broadcasted_iota(jnp.int32, (HEADS_PER_GROUP, 1, gw), 0)
    own = lane_head == head_id
    out_head = lax.broadcasted_iota(jnp.int32, (GRID_W, gw), 1) // NA_HEAD_DIM
    first_row = pl.program_id(0) * rows_per_step
    win0 = jnp.clip(first_row - NA_KH // 2, 0, n_rows - win_rows)
    chain = 0
    for j in range(rows_per_step):
        r = first_row + j
        r0 = jnp.clip(r - NA_KH // 2, 0, n_rows - NA_KH)
        off = r0 - win0
        d = r0 - r + NA_KH - 1
        for bi in range(n_batch):
            for g in range(n_groups):
                slot = chain % n_slots
                chain += 1
                lo, hi = g * gw, (g + 1) * gw
                qg = q_ref[bi, j, :, lo:hi]
                zero = jnp.zeros_like(qg)
                qm = jnp.where(own, qg[None], zero[None]).reshape(HEADS_PER_GROUP * GRID_W, gw)
                kw = k_ref[bi, pl.ds(off, NA_KH), :, lo:hi].reshape(NA_KH * GRID_W, gw)
                vw = v_ref[bi, pl.ds(off, NA_KH), :, lo:hi].reshape(NA_KH * GRID_W, gw)
                s_ref[slot] = lax.dot_general(qm, kw, (((1,), (1,)), ((), ())), preferred_element_type=F32)
                inv_l = []
                for h in range(HEADS_PER_GROUP):
                    rows = slice(h * GRID_W, (h + 1) * GRID_W)
                    bias = jnp.concatenate([bias_ref[g * HEADS_PER_GROUP + h, d + 2 * pair]
                                            for pair in range(NA_KH // 2)], axis=1)
                    s = s_ref[slot, rows, :] + bias
                    p = jnp.exp2(s - jnp.max(s, axis=-1, keepdims=True))
                    inv_l.append(1.0 / jnp.sum(p, axis=-1, keepdims=True))
                    p_ref[slot, rows, :] = p.astype(BF16)
                pv = jnp.dot(p_ref[slot], vw, preferred_element_type=F32)
                out = pv[:GRID_W] * inv_l[0]
                for h in range(1, HEADS_PER_GROUP):
                    out = jnp.where(out_head == h, pv[h * GRID_W:(h + 1) * GRID_W] * inv_l[h], out)
                o_ref[bi, j, :, lo:hi] = out.astype(BF16)


def _neighbourhood_attention(proj4, rpb):
    b, rows, w, _ = proj4.shape
    assert w == GRID_W and rows >= NA_KH
    n_heads, n_dr, n_dc = rpb.shape
    rpb_rows = jnp.pad(rpb.astype(F32), ((0, 0), (0, 0), (RPB_SHIFT, 2 * GRID_W - RPB_SHIFT - n_dc)))
    rps = next(c for c in (NA_ROWS_PER_STEP, 2, 1) if rows % c == 0 and rows >= c + NA_KH - 1)
    win_rows = rps + NA_KH - 1
    n_stack = HEADS_PER_GROUP * GRID_W
    n_keys = NA_KH * GRID_W

    def window(col_block):
        shape = (pl.Element(b), pl.Element(win_rows), pl.Element(GRID_W), pl.Element(D_ATTN))
        return pl.BlockSpec(
            shape, lambda i: (0, jnp.clip(i * rps - NA_KH // 2, 0, rows - win_rows), 0, col_block * D_ATTN))

    return pl.pallas_call(
        functools.partial(_na_kernel, rows),
        grid=(rows // rps,),
        in_specs=[pl.BlockSpec((b, rps, GRID_W, D_ATTN), lambda i: (0, i, 0, 1)), window(2), window(3),
                  pl.BlockSpec(rpb_rows.shape, lambda i: (0, 0, 0))],
        out_specs=pl.BlockSpec((b, rps, GRID_W, D_ATTN), lambda i: (0, i, 0, 0)),
        out_shape=jax.ShapeDtypeStruct((b, rows, GRID_W, D_ATTN), BF16),
        scratch_shapes=[pltpu.VMEM((NA_SLOTS, n_stack, n_keys), F32),
                        pltpu.VMEM((NA_SLOTS, n_stack, n_keys), BF16),
                        pltpu.VMEM((n_heads, n_dr - 1, GRID_W, 2 * GRID_W), F32)],
        compiler_params=_params(("arbitrary",)),
        name="neighbourhood_attention",
    )(proj4, proj4, proj4, rpb_rows)


def _pack_bf16_pairs(x):
    n = x.shape[1] // 2
    hi = lax.bitcast_convert_type(x[:, :n].astype(BF16).astype(F32), jnp.uint32)
    lo = lax.bitcast_convert_type(x[:, n:].astype(BF16).astype(F32), jnp.uint32)
    return hi | (lo >> 16)


def _unpack_bf16_pairs(w):
    hi = lax.bitcast_convert_type(w & jnp.uint32(0xFFFF0000), F32)
    lo = lax.bitcast_convert_type(w << 16, F32)
    return jnp.concatenate([hi, lo], axis=1)


SLAB = D_MODEL // 2 // ROW_SPLIT


def _store_slabs(slab_refs, words):
    for p, ref in enumerate(slab_refs):
        ref[...] = words[:, p * SLAB:(p + 1) * SLAB]


def _load_slabs(slab_refs, lead=()):
    return jnp.concatenate([ref[lead + (slice(None), slice(None))] for ref in slab_refs], axis=1)


ROUTE_ROWS = ("choice0", "choice1", "gate0", "gate1", "rank0", "rank1")
EXPERT_ROWS = 16


def _top2_route(logits_t, earlier_ref, carry_ref):
    eid = lax.broadcasted_iota(jnp.int32, logits_t.shape, 0).astype(F32)
    valid = jnp.where(eid < N_EXPERTS, logits_t, -jnp.inf)
    top = []
    for _ in range(TOP_K):
        m = jnp.max(valid, axis=0, keepdims=True)
        idx = jnp.min(jnp.where(valid == m, eid, float(EXPERT_ROWS)), axis=0, keepdims=True)
        top.append((m, idx))
        valid = jnp.where(eid == idx, -jnp.inf, valid)
    (m0, i0), (m1, i1) = top
    e = jnp.exp(m1 - m0)
    g0 = 1.0 / (1.0 + e)
    g1 = e * g0
    hot0 = (eid == i0).astype(F32)
    hot1 = (eid == i1).astype(F32)
    picked = hot0 + hot1
    carry = carry_ref[:, 0:1]
    earlier = jnp.dot(picked.astype(BF16), earlier_ref[...], preferred_element_type=F32) + carry
    r0 = jnp.sum(earlier * hot0, axis=0, keepdims=True)
    r1 = jnp.sum(earlier * hot1, axis=0, keepdims=True)
    row = lax.broadcasted_iota(jnp.int32, (SUBLANES, logits_t.shape[1]), 0)
    record = jnp.zeros((SUBLANES, logits_t.shape[1]), F32)
    for pos, val in enumerate((i0, i1, g0, g1, r0, r1)):
        record = jnp.where(row == pos, val, record)
    return record, carry + jnp.sum(picked, axis=1, keepdims=True)


def _out_proj_kernel(with_router, seq_len, cur_ref, prev_ref, next_ref, band_ref, pw_ref, ps_ref,
                     ya_ref, w_ref, h_ref, gpost_ref, gpre_ref, *refs):
    win_ref, refs = refs[-1], refs[:-1]
    y_pool = _pool_tile(seq_len, cur_ref, prev_ref, next_ref, band_ref, pw_ref, ps_ref, win_ref)
    mix = jnp.dot(y_pool, w_ref[:D_POOL, :], preferred_element_type=F32)
    mix = mix + jnp.dot(ya_ref[...], w_ref[D_POOL:, :], preferred_element_type=F32)
    h1 = h_ref[...] + _rms(mix, gpost_ref[...])
    if with_router:
        hn = _rms(h1, gpre_ref[...])
        wr_ref, earlier_ref, h1_ref, route_ref, counts_ref = refs[:5]
        slab_refs, carry_ref = refs[5:-1], refs[-1]
        _store_slabs(slab_refs, _pack_bf16_pairs(hn))

        @pl.when(pl.program_id(0) == 0)
        def _():
            carry_ref[...] = jnp.zeros_like(carry_ref)

        logits_t = lax.dot_general(wr_ref[...], hn.astype(BF16), (((1,), (1,)), ((), ())),
                                   preferred_element_type=F32)
        record, counts = _top2_route(logits_t, earlier_ref, carry_ref)
        route_ref[...] = record
        carry_ref[...] = jnp.broadcast_to(counts, carry_ref.shape)
        counts_ref[...] = jnp.broadcast_to(counts, counts_ref.shape)
    else:
        (h1_ref,) = refs
    h1_ref[...] = h1


def _out_proj(proj, seq_len, pool_w_bf16, pool_scale, y_na, w_out_bf16, h, g_post, g_pre, router_bf16=None):
    t = h.shape[0]
    tm = min(ROW_TILE, seq_len)
    assert seq_len % tm == 0 and tm % POOL_HALO == 0
    with_router = router_bf16 is not None
    row = lambda width: pl.BlockSpec((tm, width), lambda i: (i, 0))
    full = lambda *shape: pl.BlockSpec(shape, lambda i: (0,) * len(shape))
    hb = tm // POOL_HALO
    n_halo = t // POOL_HALO
    bands = jnp.asarray(_pool_bands(), BF16)
    halo_specs = [pl.BlockSpec((POOL_HALO, D_POOL), lambda i: (jnp.maximum(i * hb - 1, 0), 0)),
                  pl.BlockSpec((POOL_HALO, D_POOL), lambda i: (jnp.minimum((i + 1) * hb, n_halo - 1), 0))]
    in_specs = [row(D_POOL)] + halo_specs + [full(*bands.shape), full(*pool_w_bf16.shape), full(1, D_POOL)]
    in_specs += [row(D_ATTN), full(D_MODEL, D_MODEL), row(D_MODEL), full(1, D_MODEL), full(1, D_MODEL)]
    args = [proj, proj, proj, bands, pool_w_bf16, pool_scale.reshape(1, D_POOL),
            y_na, w_out_bf16, h, g_post.reshape(1, D_MODEL), g_pre.reshape(1, D_MODEL)]
    scratch = []
    if with_router:
        earlier_tokens = jnp.asarray(np.triu(np.ones((tm, tm), np.float32), 1), BF16)
        in_specs += [full(EXPERT_ROWS, D_MODEL), full(tm, tm)]
        args += [router_bf16, earlier_tokens]
        out_specs = [row(D_MODEL), pl.BlockSpec((SUBLANES, tm), lambda i: (0, i)), full(EXPERT_ROWS, LANES)]
        out_specs += [row(SLAB)] * ROW_SPLIT
        out_shape = [jax.ShapeDtypeStruct((t, D_MODEL), F32), jax.ShapeDtypeStruct((SUBLANES, t), F32),
                     jax.ShapeDtypeStruct((EXPERT_ROWS, LANES), F32)]
        out_shape += [jax.ShapeDtypeStruct((t, SLAB), jnp.uint32)] * ROW_SPLIT
        scratch = [pltpu.VMEM((EXPERT_ROWS, LANES), F32)]
    else:
        out_specs = [row(D_MODEL)]
        out_shape = [jax.ShapeDtypeStruct((t, D_MODEL), F32)]
    return pl.pallas_call(
        functools.partial(_out_proj_kernel, with_router, seq_len),
        grid=(t // tm,),
        in_specs=in_specs,
        out_specs=out_specs,
        out_shape=out_shape,
        scratch_shapes=scratch + [pltpu.VMEM((tm + 2 * POOL_HALO, D_POOL), BF16)],
        compiler_params=_params(("arbitrary",) if with_router else ("parallel",)),
        name="out_proj_router" if with_router else "out_proj",
    )(*args)


def _swiglu_kernel(dense, be_ref, nused_ref, nvalid_ref, *refs):
    if dense:
        wg_ref, wu_ref, wd_ref, h_ref, gpre_ref, g_ref, o_ref, acc_ref = refs
    else:
        x_refs, (wg_ref, wu_ref, wd_ref) = refs[:ROW_SPLIT], refs[ROW_SPLIT:ROW_SPLIT + 3]
        o_refs, acc_ref = refs[ROW_SPLIT + 3:-1], refs[-1]
    j = pl.program_id(0)
    f = pl.program_id(1)
    live = j < nused_ref[0]

    @pl.when((j == 0) & (f == 0))
    def _():
        acc_ref[...] = jnp.zeros_like(acc_ref)

    @pl.when(live)
    def _():
        if dense:
            x = _rms(h_ref[...], gpre_ref[...]).astype(BF16)
        else:
            row = lax.broadcasted_iota(jnp.int32, (acc_ref.shape[0], 1), 0)
            words = jnp.where(row < nvalid_ref[j], _load_slabs(x_refs), jnp.uint32(0))
            x = _unpack_bf16_pairs(words).astype(BF16)
        chunk = x.shape[0] // FFN_ROW_CHUNKS
        for c in range(FFN_ROW_CHUNKS):
            rows = slice(c * chunk, (c + 1) * chunk)
            a = jnp.dot(x[rows], wg_ref[0], preferred_element_type=F32)
            b = jnp.dot(x[rows], wu_ref[0], preferred_element_type=F32)
            hmid = (a * jax.nn.sigmoid(a) * b).astype(BF16)
            part = jnp.dot(hmid, wd_ref[0], preferred_element_type=F32)
            acc_ref[rows, :] = jnp.where(f == 0, part, acc_ref[rows, :] + part)

    @pl.when(f == pl.num_programs(1) - 1)
    def _():
        @pl.when(live)
        def _():
            if dense:
                o_ref[...] = h_ref[...] + _rms(acc_ref[...], g_ref[...])
            else:
                _store_slabs(o_refs, _pack_bf16_pairs(acc_ref[...]))

        @pl.when(jnp.logical_not(live))
        def _():
            for ref in ([o_ref] if dense else o_refs):
                ref[...] = jnp.zeros_like(ref)


def _swiglu(x, wg, wu, wd, block_expert, n_used, n_valid, g_pre=None, g_post=None):
    dense = g_pre is not None
    xs = [] if dense else list(x)
    rows = x.shape[0] if dense else xs[0].shape[0]
    dff = wg.shape[-1]
    bm = min(FFN_ROWS, rows)
    tf = max(c for c in range(MXU_TILE, min(FFN_COLS_MAX, dff) + 1, MXU_TILE) if dff % c == 0)
    assert rows % bm == 0 and dff % tf == 0
    nf = dff // tf

    def f_eff(j, f, nu):
        return jnp.where(j < nu[0], f, nf - 1)

    row_spec = lambda width: pl.BlockSpec((bm, width), lambda j, f, be, nu, nv: (j, 0))
    in_specs = [row_spec(a.shape[1]) for a in xs] + [
        pl.BlockSpec((1, D_MODEL, tf), lambda j, f, be, nu, nv: (be[j], 0, f_eff(j, f, nu))),
        pl.BlockSpec((1, D_MODEL, tf), lambda j, f, be, nu, nv: (be[j], 0, f_eff(j, f, nu))),
        pl.BlockSpec((1, tf, D_MODEL), lambda j, f, be, nu, nv: (be[j], f_eff(j, f, nu), 0)),
    ]
    args = xs + [wg, wu, wd]
    if dense:
        gain_spec = pl.BlockSpec((1, D_MODEL), lambda j, f, be, nu, nv: (0, 0))
        in_specs += [row_spec(D_MODEL), gain_spec, gain_spec]
        args += [x, g_pre.reshape(1, D_MODEL), g_post.reshape(1, D_MODEL)]
        out_shape, out_specs = jax.ShapeDtypeStruct((rows, D_MODEL), F32), row_spec(D_MODEL)
    else:
        out_shape = [jax.ShapeDtypeStruct((rows, SLAB), jnp.uint32)] * ROW_SPLIT
        out_specs = [row_spec(SLAB)] * ROW_SPLIT
    return pl.pallas_call(
        functools.partial(_swiglu_kernel, dense),
        grid_spec=pltpu.PrefetchScalarGridSpec(
            num_scalar_prefetch=3,
            grid=(rows // bm, nf),
            in_specs=in_specs,
            out_specs=out_specs,
            scratch_shapes=[pltpu.VMEM((bm, D_MODEL), F32)],
        ),
        out_shape=out_shape,
        compiler_params=_params(("parallel", "arbitrary")),
        name="swiglu_dense" if dense else "swiglu_experts",
    )(block_expert, n_used, n_valid, *args)


def _sc_mesh():
    return plsc.VectorSubcoreMesh(core_axis_name="core", subcore_axis_name="subcore")


def _sc_dispatch(slabs, dest, n_rows):
    t, width = slabs[0].shape
    idx = [dest[k].reshape(1, t) for k in range(TOP_K)]
    n_slabs = len(slabs)

    @functools.partial(pl.kernel, mesh=_sc_mesh(), scratch_types=[], name="moe_dispatch_sc",
                       out_type=[jax.ShapeDtypeStruct((n_rows, width), slabs[0].dtype)] * n_slabs)
    def scatter_rows(*refs):
        x_hbm, idx_hbm, o_hbm = refs[:n_slabs], refs[n_slabs:n_slabs + TOP_K], refs[n_slabs + TOP_K:]
        for p in range(n_slabs):
            def body(x_vmem, *idx_vmem, out=o_hbm[p]):
                for i_vmem in idx_vmem:
                    pltpu.sync_copy(x_vmem, out.at[i_vmem.at[0]])

            pltpu.emit_pipeline(
                body,
                grid=(t // SC_WINDOW,),
                in_specs=[pl.BlockSpec((SC_WINDOW, width), lambda i: (i, 0))]
                + [pl.BlockSpec((1, SC_WINDOW), lambda i: (0, i))] * TOP_K,
                out_specs=[],
                core_axis_name=("core", "subcore"),
                dimension_semantics=(pltpu.PARALLEL,),
            )(x_hbm[p], *idx_hbm)

    return scatter_rows(*slabs, *idx)


def _sc_gather(slabs, idx):
    n = idx.shape[0]
    width = slabs[0].shape[1]
    n_slabs = len(slabs)

    @functools.partial(pl.kernel, mesh=_sc_mesh(), scratch_types=[], name="moe_gather_sc",
                       out_type=[jax.ShapeDtypeStruct((n, width), slabs[0].dtype)] * n_slabs)
    def gather_rows(*refs):
        x_hbm, i_hbm, o_hbm = refs[:n_slabs], refs[n_slabs], refs[n_slabs + 1:]
        for p in range(n_slabs):
            def body(i_vmem, o_vmem, src=x_hbm[p]):
                pltpu.sync_copy(src.at[i_vmem.at[0]], o_vmem)

            pltpu.emit_pipeline(
                body,
                grid=(n // SC_WINDOW,),
                in_specs=[pl.BlockSpec((1, SC_WINDOW), lambda i: (0, i))],
                out_specs=[pl.BlockSpec((SC_WINDOW, width), lambda i: (i, 0))],
                core_axis_name=("core", "subcore"),
                dimension_semantics=(pltpu.PARALLEL,),
            )(i_hbm, o_hbm[p])

    return gather_rows(*slabs, idx.reshape(1, n))


def _combine_kernel(*refs):
    y_refs, (route_ref, h_ref, g_ref, o_ref) = refs[:ROW_SPLIT], refs[ROW_SPLIT:]
    per_token = route_ref[...].T
    gate0 = ROUTE_ROWS.index("gate0")
    f = _unpack_bf16_pairs(_load_slabs(y_refs, (0,))) * per_token[:, gate0:gate0 + 1]
    for k in range(1, TOP_K):
        f = f + _unpack_bf16_pairs(_load_slabs(y_refs, (k,))) * per_token[:, gate0 + k:gate0 + k + 1]
    o_ref[...] = h_ref[...] + _rms(f, g_ref[...])


def _combine(y_slabs, record, h, g_post):
    t = h.shape[0]
    tm = min(ROW_TILE, t)
    return pl.pallas_call(
        _combine_kernel,
        grid=(t // tm,),
        in_specs=[pl.BlockSpec((TOP_K, tm, SLAB), lambda i: (0, i, 0))] * ROW_SPLIT + [
            pl.BlockSpec((SUBLANES, tm), lambda i: (0, i)),
            pl.BlockSpec((tm, D_MODEL), lambda i: (i, 0)),
            pl.BlockSpec((1, D_MODEL), lambda i: (0, 0)),
        ],
        out_specs=pl.BlockSpec((tm, D_MODEL), lambda i: (i, 0)),
        out_shape=jax.ShapeDtypeStruct(h.shape, F32),
        input_output_aliases={ROW_SPLIT + 1: 0},
        compiler_params=_params(("parallel",)),
        name="moe_combine",
    )(*y_slabs, record, h, g_post.reshape(1, D_MODEL))


def _route(record, counts, bm):
    t = record.shape[1]
    rows = {name: record[pos] for pos, name in enumerate(ROUTE_ROWS)}
    counts = counts[:N_EXPERTS, 0].astype(jnp.int32)
    padded = (counts + bm - 1) // bm * bm
    pad_end = jnp.cumsum(padded)
    pad_start = pad_end - padded
    dest = []
    for k in range(TOP_K):
        choice = rows[f"choice{k}"].astype(jnp.int32)
        start = sum(jnp.where(choice == e, pad_start[e], 0) for e in range(N_EXPERTS))
        dest.append(start + rows[f"rank{k}"].astype(jnp.int32))
    dest = jnp.stack(dest, axis=0)
    n_blocks = (t * TOP_K) // bm + N_EXPERTS
    n_used = (pad_end[-1] // bm).astype(jnp.int32)
    blk = jnp.arange(n_blocks, dtype=jnp.int32)
    last = jnp.minimum(blk, n_used - 1) * bm
    block_expert = jnp.sum((last[:, None] >= pad_end[None, :]).astype(jnp.int32), axis=1)
    block_expert = jnp.minimum(block_expert, N_EXPERTS - 1)
    n_valid = jnp.clip(pad_start[block_expert] + counts[block_expert] - blk * bm, 0, bm)
    n_valid = jnp.where(blk < n_used, n_valid, 0).astype(jnp.int32)
    return dest, block_expert, n_used.reshape(1), n_valid, n_blocks * bm


def kernel(x, mix_norm_pre, mix_norm_post, ffn_norm_pre, ffn_norm_post, w_in, pool_w, pool_scale, na_rpb,
           w_out, dense_w_gate, dense_w_up, dense_w_down, moe_router, moe_w_gate, moe_w_up, moe_w_down):
    b, s, d = x.shape
    assert d == D_MODEL and s % GRID_W == 0
    t = b * s
    rows = s // GRID_W
    depth = w_in.shape[0]
    h = x.reshape(t, d)
    for layer in range(depth):
        j = layer // 2
        is_moe = layer % 2 == 1
        proj = _norm_proj(h, mix_norm_pre[layer], w_in[layer].astype(BF16))
        y_na = _neighbourhood_attention(proj.reshape(b, rows, GRID_W, D_IN), na_rpb[layer])
        router = None
        if is_moe:
            router = jnp.pad(moe_router[j].T, ((0, EXPERT_ROWS - N_EXPERTS), (0, 0))).astype(BF16)
        outs = _out_proj(proj, s, pool_w[layer].astype(BF16), pool_scale[layer], y_na.reshape(t, D_ATTN),
                         w_out[layer].astype(BF16), h, mix_norm_post[layer], ffn_norm_pre[layer], router)
        if is_moe:
            h1, record, counts = outs[:3]
            dest, block_expert, n_used, n_valid, n_rows = _route(record, counts, min(FFN_ROWS, t))
            xs = _sc_dispatch(list(outs[3:]), dest, n_rows)
            ys = _swiglu(xs, moe_w_gate[j].astype(BF16), moe_w_up[j].astype(BF16),
                         moe_w_down[j].astype(BF16), block_expert, n_used, n_valid)
            by_choice = _sc_gather(list(ys), dest.reshape(TOP_K * t))
            h = _combine([y.reshape(TOP_K, t, SLAB) for y in by_choice], record, h1, ffn_norm_post[layer])
        else:
            (h1,) = outs
            bm = min(FFN_ROWS, t)
            n_blk = t // bm
            h = _swiglu(h1, dense_w_gate[j][None].astype(BF16), dense_w_up[j][None].astype(BF16),
                        dense_w_down[j][None].astype(BF16), jnp.zeros((n_blk,), jnp.int32),
                        jnp.full((1,), n_blk, jnp.int32), jnp.full((n_blk,), bm, jnp.int32),
                        g_pre=ffn_norm_pre[layer], g_post=ffn_norm_post[layer])
    return h.reshape(b, s, d)
```

```python
import functools

import jax
import jax.numpy as jnp
import numpy as np
from jax import lax
from jax.experimental import pallas as pl
from jax.experimental.pallas import tpu as pltpu
from jax.experimental.pallas import tpu_sc as plsc

F32 = jnp.float32
BF16 = jnp.bfloat16

D_MODEL = 1024
D_POOL = 512
POOL_WINDOWS = (2, 4, 8, 16)
POOL_GROUP_DIM = 128
D_ATTN = 512
NA_HEAD_DIM = 32
NA_HEADS = 16
D_IN = D_POOL + 3 * D_ATTN
GRID_W = 64
NA_KH = 8
NA_KW = 16
N_EXPERTS = 8
TOP_K = 2
RMS_EPS = 1e-6
NEG_BIG = -1e30
LOG2E = 1.4426950408889634

LANES = 128
SUBLANES = 8
HEADS_PER_GROUP = 256 // NA_HEAD_DIM
VMEM_LIMIT = 52 * 1024 * 1024

ROW_TILE = 1024
POOL_SUB = 128
POOL_HALO = 64
FFN_ROWS = 512
FFN_ROW_CHUNKS = 2
FFN_COLS_MAX = 2048
MXU_TILE = 256
NA_ROWS_PER_STEP = 4
NA_SLOTS = 8
SC_LANES = 16
SC_CAST_COLS = 512
SC_WINDOW = 128
ROW_SPLIT = 2


def _rms(x, g):
    ms = jnp.mean(x * x, axis=-1, keepdims=True)
    return x * lax.rsqrt(ms + RMS_EPS) * g


def _params(sem):
    return pltpu.CompilerParams(dimension_semantics=sem, vmem_limit_bytes=VMEM_LIMIT)


def _norm_proj_kernel(x_ref, g_ref, w_ref, o_ref):
    hn = _rms(x_ref[...], g_ref[...])
    p = jnp.dot(hn.astype(BF16), w_ref[...], preferred_element_type=F32)
    q_lo, q_hi = D_POOL, D_POOL + D_ATTN
    o_ref[:, :q_lo] = p[:, :q_lo].astype(BF16)
    o_ref[:, q_lo:q_hi] = (p[:, q_lo:q_hi] * (NA_HEAD_DIM ** -0.5 * LOG2E)).astype(BF16)
    o_ref[:, q_hi:] = p[:, q_hi:].astype(BF16)


def _norm_proj(h, g, w_bf16):
    t = h.shape[0]
    tm = min(ROW_TILE, t)
    return pl.pallas_call(
        _norm_proj_kernel,
        grid=(t // tm,),
        in_specs=[
            pl.BlockSpec((tm, D_MODEL), lambda i: (i, 0)),
            pl.BlockSpec((1, D_MODEL), lambda i: (0, 0)),
            pl.BlockSpec((D_MODEL, D_IN), lambda i: (0, 0)),
        ],
        out_specs=pl.BlockSpec((tm, D_IN), lambda i: (i, 0)),
        out_shape=jax.ShapeDtypeStruct((t, D_IN), BF16),
        compiler_params=_params(("parallel",)),
        name="norm_proj",
    )(h, g.reshape(1, D_MODEL), w_bf16)


def _pool_bands():
    kdim = POOL_SUB + 2 * POOL_HALO
    rel = np.arange(kdim)[None, :] - np.arange(POOL_SUB)[:, None] - POOL_HALO
    return np.stack([((rel >= -(w // 2)) & (rel < w - w // 2)) for w in POOL_WINDOWS]).astype(np.float32)


def _pool_tile(seq_len, cur_ref, prev_ref, next_ref, band_ref, pw_ref, ps_ref, win_ref):
    tm = cur_ref.shape[0]
    tiles_per_seq = seq_len // tm
    i = pl.program_id(0) % tiles_per_seq
    t0 = i * tm
    halo_zero = jnp.zeros((POOL_HALO, D_POOL), BF16)
    win_ref[:POOL_HALO, :] = jnp.where(i == 0, halo_zero, prev_ref[...])
    win_ref[POOL_HALO:POOL_HALO + tm, :] = cur_ref[...]
    win_ref[POOL_HALO + tm:, :] = jnp.where(i == tiles_per_seq - 1, halo_zero, next_ref[...])
    kdim = POOL_SUB + 2 * POOL_HALO
    t_abs = t0 + lax.broadcasted_iota(jnp.int32, (tm, 1), 0)
    groups = []
    for g, w in enumerate(POOL_WINDOWS):
        half = w // 2
        c0, c1 = g * POOL_GROUP_DIM, (g + 1) * POOL_GROUP_DIM
        cnt = (jnp.minimum(t_abs + (w - half), seq_len) - jnp.maximum(t_abs - half, 0)).astype(F32)
        wsum = jnp.concatenate(
            [jnp.dot(band_ref[g], win_ref[s * POOL_SUB:s * POOL_SUB + kdim, c0:c1], preferred_element_type=F32)
             for s in range(tm // POOL_SUB)], axis=0)
        delta = wsum / cnt - cur_ref[:, c0:c1].astype(F32)
        y = jnp.dot(delta.astype(BF16), pw_ref[g], preferred_element_type=F32)
        groups.append((y * ps_ref[:, c0:c1]).astype(BF16))
    return jnp.concatenate(groups, axis=1)


RPB_SHIFT = GRID_W - NA_KW


def _na_build_bias(rpb_ref, bias_ref):
    n_dr = rpb_ref.shape[1]
    c = lax.broadcasted_iota(jnp.int32, (GRID_W, 2 * GRID_W), 0)
    lane = lax.broadcasted_iota(jnp.int32, (GRID_W, 2 * GRID_W), 1)
    kc = lane % GRID_W
    c0 = jnp.clip(c - NA_KW // 2, 0, GRID_W - NA_KW)
    inside = (kc >= c0) & (kc < c0 + NA_KW)
    first_half = lane < GRID_W

    def per_head(h, carry):
        halves = []
        for dr in range(n_dr):
            row = jnp.broadcast_to(rpb_ref[h, dr:dr + 1, :], (GRID_W, 2 * GRID_W)) * LOG2E
            halves.append((pltpu.roll(row, GRID_W + 1, 1, stride=1, stride_axis=0),
                           pltpu.roll(row, 1, 1, stride=1, stride_axis=0)))
        for e in range(n_dr - 1):
            pair = jnp.where(first_half, halves[e][0], halves[e + 1][1])
            bias_ref[h, e] = jnp.where(inside, pair, NEG_BIG)
        return carry

    lax.fori_loop(0, rpb_ref.shape[0], per_head, 0)


def _na_kernel(n_rows, q_ref, k_ref, v_ref, rpb_ref, o_ref, s_ref, p_ref, bias_ref):
    @pl.when(pl.program_id(0) == 0)
    def _():
        _na_build_bias(rpb_ref, bias_ref)

    n_batch, rows_per_step = q_ref.shape[0], q_ref.shape[1]
    win_rows = k_ref.shape[1]
    gw = HEADS_PER_GROUP * NA_HEAD_DIM
    n_groups = NA_HEADS // HEADS_PER_GROUP
    n_slots = s_ref.shape[0]
    lane_head = lax.broadcasted_iota(jnp.int32, (HEADS_PER_GROUP, 1, gw), 2) // NA_HEAD_DIM
    head_id = lax.broadcasted_iota(jnp.int32, (HEADS_PER_GROUP, 1, gw), 0)
    own = lane_head == head_id
    out_head = lax.broadcasted_iota(jnp.int32, (GRID_W, gw), 1) // NA_HEAD_DIM
    first_row = pl.program_id(0) * rows_per_step
    win0 = jnp.clip(first_row - NA_KH // 2, 0, n_rows - win_rows)
    chain = 0
    for j in range(rows_per_step):
        r = first_row + j
        r0 = jnp.clip(r - NA_KH // 2, 0, n_rows - NA_KH)
        off = r0 - win0
        d = r0 - r + NA_KH - 1
        for bi in range(n_batch):
            for g in range(n_groups):
                slot = chain % n_slots
                chain += 1
                lo, hi = g * gw, (g + 1) * gw
                qg = q_ref[bi, j, :, lo:hi]
                zero = jnp.zeros_like(qg)
                qm = jnp.where(own, qg[None], zero[None]).reshape(HEADS_PER_GROUP * GRID_W, gw)
                kw = k_ref[bi, pl.ds(off, NA_KH), :, lo:hi].reshape(NA_KH * GRID_W, gw)
                vw = v_ref[bi, pl.ds(off, NA_KH), :, lo:hi].reshape(NA_KH * GRID_W, gw)
                s_ref[slot] = lax.dot_general(qm, kw, (((1,), (1,)), ((), ())), preferred_element_type=F32)
                inv_l = []
                for h in range(HEADS_PER_GROUP):
                    rows = slice(h * GRID_W, (h + 1) * GRID_W)
                    bias = jnp.concatenate([bias_ref[g * HEADS_PER_GROUP + h, d + 2 * pair]
                                            for pair in range(NA_KH // 2)], axis=1)
                    s = s_ref[slot, rows, :] + bias
                    p = jnp.exp2(s - jnp.max(s, axis=-1, keepdims=True))
                    inv_l.append(1.0 / jnp.sum(p, axis=-1, keepdims=True))
                    p_ref[slot, rows, :] = p.astype(BF16)
                pv = jnp.dot(p_ref[slot], vw, preferred_element_type=F32)
                out = pv[:GRID_W] * inv_l[0]
                for h in range(1, HEADS_PER_GROUP):
                    out = jnp.where(out_head == h, pv[h * GRID_W:(h + 1) * GRID_W] * inv_l[h], out)
                o_ref[bi, j, :, lo:hi] = out.astype(BF16)


def _neighbourhood_attention(proj4, rpb):
    b, rows, w, _ = proj4.shape
    assert w == GRID_W and rows >= NA_KH
    n_heads, n_dr, n_dc = rpb.shape
    rpb_rows = jnp.pad(rpb.astype(F32), ((0, 0), (0, 0), (RPB_SHIFT, 2 * GRID_W - RPB_SHIFT - n_dc)))
    rps = next(c for c in (NA_ROWS_PER_STEP, 2, 1) if rows % c == 0 and rows >= c + NA_KH - 1)
    win_rows = rps + NA_KH - 1
    n_stack = HEADS_PER_GROUP * GRID_W
    n_keys = NA_KH * GRID_W

    def window(col_block):
        shape = (pl.Element(b), pl.Element(win_rows), pl.Element(GRID_W), pl.Element(D_ATTN))
        return pl.BlockSpec(
            shape, lambda i: (0, jnp.clip(i * rps - NA_KH // 2, 0, rows - win_rows), 0, col_block * D_ATTN))

    return pl.pallas_call(
        functools.partial(_na_kernel, rows),
        grid=(rows // rps,),
        in_specs=[pl.BlockSpec((b, rps, GRID_W, D_ATTN), lambda i: (0, i, 0, 1)), window(2), window(3),
                  pl.BlockSpec(rpb_rows.shape, lambda i: (0, 0, 0))],
        out_specs=pl.BlockSpec((b, rps, GRID_W, D_ATTN), lambda i: (0, i, 0, 0)),
        out_shape=jax.ShapeDtypeStruct((b, rows, GRID_W, D_ATTN), BF16),
        scratch_shapes=[pltpu.VMEM((NA_SLOTS, n_stack, n_keys), F32),
                        pltpu.VMEM((NA_SLOTS, n_stack, n_keys), BF16),
                        pltpu.VMEM((n_heads, n_dr - 1, GRID_W, 2 * GRID_W), F32)],
        compiler_params=_params(("arbitrary",)),
        name="neighbourhood_attention",
    )(proj4, proj4, proj4, rpb_rows)


def _pack_bf16_pairs(x):
    n = x.shape[1] // 2
    hi = lax.bitcast_convert_type(x[:, :n].astype(BF16).astype(F32), jnp.uint32)
    lo = lax.bitcast_convert_type(x[:, n:].astype(BF16).astype(F32), jnp.uint32)
    return hi | (lo >> 16)


def _unpack_bf16_pairs(w):
    hi = lax.bitcast_convert_type(w & jnp.uint32(0xFFFF0000), F32)
    lo = lax.bitcast_convert_type(w << 16, F32)
    return jnp.concatenate([hi, lo], axis=1)


SLAB = D_MODEL // 2 // ROW_SPLIT


def _store_slabs(slab_refs, words):
    for p, ref in enumerate(slab_refs):
        ref[...] = words[:, p * SLAB:(p + 1) * SLAB]


def _load_slabs(slab_refs, lead=()):
    return jnp.concatenate([ref[lead + (slice(None), slice(None))] for ref in slab_refs], axis=1)


ROUTE_ROWS = ("choice0", "choice1", "gate0", "gate1", "rank0", "rank1")
EXPERT_ROWS = 16


def _top2_route(logits_t, earlier_ref, carry_ref):
    eid = lax.broadcasted_iota(jnp.int32, logits_t.shape, 0).astype(F32)
    valid = jnp.where(eid < N_EXPERTS, logits_t, -jnp.inf)
    top = []
    for _ in range(TOP_K):
        m = jnp.max(valid, axis=0, keepdims=True)
        idx = jnp.min(jnp.where(valid == m, eid, float(EXPERT_ROWS)), axis=0, keepdims=True)
        top.append((m, idx))
        valid = jnp.where(eid == idx, -jnp.inf, valid)
    (m0, i0), (m1, i1) = top
    e = jnp.exp(m1 - m0)
    g0 = 1.0 / (1.0 + e)
    g1 = e * g0
    hot0 = (eid == i0).astype(F32)
    hot1 = (eid == i1).astype(F32)
    picked = hot0 + hot1
    carry = carry_ref[:, 0:1]
    earlier = jnp.dot(picked.astype(BF16), earlier_ref[...], preferred_element_type=F32) + carry
    r0 = jnp.sum(earlier * hot0, axis=0, keepdims=True)
    r1 = jnp.sum(earlier * hot1, axis=0, keepdims=True)
    row = lax.broadcasted_iota(jnp.int32, (SUBLANES, logits_t.shape[1]), 0)
    record = jnp.zeros((SUBLANES, logits_t.shape[1]), F32)
    for pos, val in enumerate((i0, i1, g0, g1, r0, r1)):
        record = jnp.where(row == pos, val, record)
    return record, carry + jnp.sum(picked, axis=1, keepdims=True)


def _out_proj_kernel(with_router, seq_len, cur_ref, prev_ref, next_ref, band_ref, pw_ref, ps_ref,
                     ya_ref, w_ref, h_ref, gpost_ref, gpre_ref, *refs):
    win_ref, refs = refs[-1], refs[:-1]
    y_pool = _pool_tile(seq_len, cur_ref, prev_ref, next_ref, band_ref, pw_ref, ps_ref, win_ref)
    mix = jnp.dot(y_pool, w_ref[:D_POOL, :], preferred_element_type=F32)
    mix = mix + jnp.dot(ya_ref[...], w_ref[D_POOL:, :], preferred_element_type=F32)
    h1 = h_ref[...] + _rms(mix, gpost_ref[...])
    if with_router:
        hn = _rms(h1, gpre_ref[...])
        wr_ref, earlier_ref, h1_ref, route_ref, counts_ref = refs[:5]
        slab_refs, carry_ref = refs[5:-1], refs[-1]
        _store_slabs(slab_refs, _pack_bf16_pairs(hn))

        @pl.when(pl.program_id(0) == 0)
        def _():
            carry_ref[...] = jnp.zeros_like(carry_ref)

        logits_t = lax.dot_general(wr_ref[...], hn.astype(BF16), (((1,), (1,)), ((), ())),
                                   preferred_element_type=F32)
        record, counts = _top2_route(logits_t, earlier_ref, carry_ref)
        route_ref[...] = record
        carry_ref[...] = jnp.broadcast_to(counts, carry_ref.shape)
        counts_ref[...] = jnp.broadcast_to(counts, counts_ref.shape)
    else:
        (h1_ref,) = refs
    h1_ref[...] = h1


def _out_proj(proj, seq_len, pool_w_bf16, pool_scale, y_na, w_out_bf16, h, g_post, g_pre, router_bf16=None):
    t = h.shape[0]
    tm = min(ROW_TILE, seq_len)
    assert seq_len % tm == 0 and tm % POOL_HALO == 0
    with_router = router_bf16 is not None
    row = lambda width: pl.BlockSpec((tm, width), lambda i: (i, 0))
    full = lambda *shape: pl.BlockSpec(shape, lambda i: (0,) * len(shape))
    hb = tm // POOL_HALO
    n_halo = t // POOL_HALO
    bands = jnp.asarray(_pool_bands(), BF16)
    halo_specs = [pl.BlockSpec((POOL_HALO, D_POOL), lambda i: (jnp.maximum(i * hb - 1, 0), 0)),
                  pl.BlockSpec((POOL_HALO, D_POOL), lambda i: (jnp.minimum((i + 1) * hb, n_halo - 1), 0))]
    in_specs = [row(D_POOL)] + halo_specs + [full(*bands.shape), full(*pool_w_bf16.shape), full(1, D_POOL)]
    in_specs += [row(D_ATTN), full(D_MODEL, D_MODEL), row(D_MODEL), full(1, D_MODEL), full(1, D_MODEL)]
    args = [proj, proj, proj, bands, pool_w_bf16, pool_scale.reshape(1, D_POOL),
            y_na, w_out_bf16, h, g_post.reshape(1, D_MODEL), g_pre.reshape(1, D_MODEL)]
    scratch = []
    if with_router:
        earlier_tokens = jnp.asarray(np.triu(np.ones((tm, tm), np.float32), 1), BF16)
        in_specs += [full(EXPERT_ROWS, D_MODEL), full(tm, tm)]
        args += [router_bf16, earlier_tokens]
        out_specs = [row(D_MODEL), pl.BlockSpec((SUBLANES, tm), lambda i: (0, i)), full(EXPERT_ROWS, LANES)]
        out_specs += [row(SLAB)] * ROW_SPLIT
        out_shape = [jax.ShapeDtypeStruct((t, D_MODEL), F32), jax.ShapeDtypeStruct((SUBLANES, t), F32),
                     jax.ShapeDtypeStruct((EXPERT_ROWS, LANES), F32)]
        out_shape += [jax.ShapeDtypeStruct((t, SLAB), jnp.uint32)] * ROW_SPLIT
        scratch = [pltpu.VMEM((EXPERT_ROWS, LANES), F32)]
    else:
        out_specs = [row(D_MODEL)]
        out_shape = [jax.ShapeDtypeStruct((t, D_MODEL), F32)]
    return pl.pallas_call(
        functools.partial(_out_proj_kernel, with_router, seq_len),
        grid=(t // tm,),
        in_specs=in_specs,
        out_specs=out_specs,
        out_shape=out_shape,
        scratch_shapes=scratch + [pltpu.VMEM((tm + 2 * POOL_HALO, D_POOL), BF16)],
        compiler_params=_params(("arbitrary",) if with_router else ("parallel",)),
        name="out_proj_router" if with_router else "out_proj",
    )(*args)


def _swiglu_kernel(dense, be_ref, nused_ref, nvalid_ref, *refs):
    if dense:
        wg_ref, wu_ref, wd_ref, h_ref, gpre_ref, g_ref, o_ref, acc_ref = refs
    else:
        x_refs, (wg_ref, wu_ref, wd_ref) = refs[:ROW_SPLIT], refs[ROW_SPLIT:ROW_SPLIT + 3]
        o_refs, acc_ref = refs[ROW_SPLIT + 3:-1], refs[-1]
    j = pl.program_id(0)
    f = pl.program_id(1)
    live = j < nused_ref[0]

    @pl.when((j == 0) & (f == 0))
    def _():
        acc_ref[...] = jnp.zeros_like(acc_ref)

    @pl.when(live)
    def _():
        if dense:
            x = _rms(h_ref[...], gpre_ref[...]).astype(BF16)
        else:
            row = lax.broadcasted_iota(jnp.int32, (acc_ref.shape[0], 1), 0)
            words = jnp.where(row < nvalid_ref[j], _load_slabs(x_refs), jnp.uint32(0))
            x = _unpack_bf16_pairs(words).astype(BF16)
        if dense:
            w_gate, w_up, w_down = wg_ref[0], wu_ref[0], wd_ref[0]
        else:
            w_gate, w_up, w_down = (pltpu.bitcast(ref[0], BF16) for ref in (wg_ref, wu_ref, wd_ref))
        chunk = x.shape[0] // FFN_ROW_CHUNKS
        for c in range(FFN_ROW_CHUNKS):
            rows = slice(c * chunk, (c + 1) * chunk)
            a = jnp.dot(x[rows], w_gate, preferred_element_type=F32)
            b = jnp.dot(x[rows], w_up, preferred_element_type=F32)
            hmid = (a * jax.nn.sigmoid(a) * b).astype(BF16)
            part = jnp.dot(hmid, w_down, preferred_element_type=F32)
            acc_ref[rows, :] = jnp.where(f == 0, part, acc_ref[rows, :] + part)

    @pl.when(f == pl.num_programs(1) - 1)
    def _():
        @pl.when(live)
        def _():
            if dense:
                o_ref[...] = h_ref[...] + _rms(acc_ref[...], g_ref[...])
            else:
                _store_slabs(o_refs, _pack_bf16_pairs(acc_ref[...]))

        @pl.when(jnp.logical_not(live))
        def _():
            for ref in ([o_ref] if dense else o_refs):
                ref[...] = jnp.zeros_like(ref)


def _swiglu(x, wg, wu, wd, block_expert, n_used, n_valid, g_pre=None, g_post=None):
    dense = g_pre is not None
    xs = [] if dense else list(x)
    rows = x.shape[0] if dense else xs[0].shape[0]
    dff = wg.shape[-1]
    bm = min(FFN_ROWS, rows)
    tf = max(c for c in range(MXU_TILE, min(FFN_COLS_MAX, dff) + 1, MXU_TILE) if dff % c == 0)
    assert rows % bm == 0 and dff % tf == 0
    nf = dff // tf

    def f_eff(j, f, nu):
        return jnp.where(j < nu[0], f, nf - 1)

    row_spec = lambda width: pl.BlockSpec((bm, width), lambda j, f, be, nu, nv: (j, 0))
    pack = 1 if dense else 2
    in_specs = [row_spec(a.shape[1]) for a in xs] + [
        pl.BlockSpec((1, D_MODEL // pack, tf), lambda j, f, be, nu, nv: (be[j], 0, f_eff(j, f, nu))),
        pl.BlockSpec((1, D_MODEL // pack, tf), lambda j, f, be, nu, nv: (be[j], 0, f_eff(j, f, nu))),
        pl.BlockSpec((1, tf // pack, D_MODEL), lambda j, f, be, nu, nv: (be[j], f_eff(j, f, nu), 0)),
    ]
    args = xs + [wg, wu, wd]
    if dense:
        gain_spec = pl.BlockSpec((1, D_MODEL), lambda j, f, be, nu, nv: (0, 0))
        in_specs += [row_spec(D_MODEL), gain_spec, gain_spec]
        args += [x, g_pre.reshape(1, D_MODEL), g_post.reshape(1, D_MODEL)]
        out_shape, out_specs = jax.ShapeDtypeStruct((rows, D_MODEL), F32), row_spec(D_MODEL)
    else:
        out_shape = [jax.ShapeDtypeStruct((rows, SLAB), jnp.uint32)] * ROW_SPLIT
        out_specs = [row_spec(SLAB)] * ROW_SPLIT
    return pl.pallas_call(
        functools.partial(_swiglu_kernel, dense),
        grid_spec=pltpu.PrefetchScalarGridSpec(
            num_scalar_prefetch=3,
            grid=(rows // bm, nf),
            in_specs=in_specs,
            out_specs=out_specs,
            scratch_shapes=[pltpu.VMEM((bm, D_MODEL), F32)],
        ),
        out_shape=out_shape,
        compiler_params=_params(("parallel", "arbitrary")),
        name="swiglu_dense" if dense else "swiglu_experts",
    )(block_expert, n_used, n_valid, *args)


def _sc_mesh():
    return plsc.VectorSubcoreMesh(core_axis_name="core", subcore_axis_name="subcore")


def _bf16_bits(v):
    u = lax.bitcast_convert_type(v, jnp.uint32)
    return (u + jnp.uint32(0x7FFF) + ((u >> 16) & jnp.uint32(1))) >> 16


def _sc_cast_pairs(w):
    n_e, n_r, n_c = w.shape
    rows = n_e * n_r
    pair_rows = 2 * SUBLANES

    @functools.partial(pl.kernel, mesh=_sc_mesh(), scratch_types=[], name="weight_cast_sc",
                       out_type=jax.ShapeDtypeStruct((rows // 2, n_c), jnp.uint32),
                       compiler_params=pltpu.CompilerParams(needs_layout_passes=False))
    def cast(x_hbm, o_hbm):
        def body(x_vmem, o_vmem):
            @pl.loop(0, SC_CAST_COLS, step=SC_LANES)
            def _(c):
                for i in range(SUBLANES):
                    pair = plsc.pack(x_vmem[2 * i, pl.ds(c, SC_LANES)], x_vmem[2 * i + 1, pl.ds(c, SC_LANES)],
                                     format=plsc.PackFormat.INTERLEAVED)
                    o_vmem[i, pl.ds(c, SC_LANES)] = plsc.bitcast(pair, jnp.uint32)

        pltpu.emit_pipeline(
            body,
            grid=(rows // pair_rows, n_c // SC_CAST_COLS),
            in_specs=[pl.BlockSpec((pair_rows, SC_CAST_COLS), lambda r, c: (r, c))],
            out_specs=[pl.BlockSpec((SUBLANES, SC_CAST_COLS), lambda r, c: (r, c))],
            core_axis_name=("core", "subcore"),
            dimension_semantics=(pltpu.PARALLEL, pltpu.PARALLEL),
        )(x_hbm, o_hbm)

    return cast(w.reshape(rows, n_c)).reshape(n_e, n_r // 2, n_c)


def _sc_dispatch(slabs, dest, n_rows):
    t, width = slabs[0].shape
    idx = [dest[k].reshape(1, t) for k in range(TOP_K)]
    n_slabs = len(slabs)

    @functools.partial(pl.kernel, mesh=_sc_mesh(), scratch_types=[], name="moe_dispatch_sc",
                       out_type=[jax.ShapeDtypeStruct((n_rows, width), slabs[0].dtype)] * n_slabs)
    def scatter_rows(*refs):
        x_hbm, idx_hbm, o_hbm = refs[:n_slabs], refs[n_slabs:n_slabs + TOP_K], refs[n_slabs + TOP_K:]
        for p in range(n_slabs):
            def body(x_vmem, *idx_vmem, out=o_hbm[p]):
                for i_vmem in idx_vmem:
                    pltpu.sync_copy(x_vmem, out.at[i_vmem.at[0]])

            pltpu.emit_pipeline(
                body,
                grid=(t // SC_WINDOW,),
                in_specs=[pl.BlockSpec((SC_WINDOW, width), lambda i: (i, 0))]
                + [pl.BlockSpec((1, SC_WINDOW), lambda i: (0, i))] * TOP_K,
                out_specs=[],
                core_axis_name=("core", "subcore"),
                dimension_semantics=(pltpu.PARALLEL,),
            )(x_hbm[p], *idx_hbm)

    return scatter_rows(*slabs, *idx)


def _sc_gather(slabs, idx):
    n = idx.shape[0]
    width = slabs[0].shape[1]
    n_slabs = len(slabs)

    @functools.partial(pl.kernel, mesh=_sc_mesh(), scratch_types=[], name="moe_gather_sc",
                       out_type=[jax.ShapeDtypeStruct((n, width), slabs[0].dtype)] * n_slabs)
    def gather_rows(*refs):
        x_hbm, i_hbm, o_hbm = refs[:n_slabs], refs[n_slabs], refs[n_slabs + 1:]
        for p in range(n_slabs):
            def body(i_vmem, o_vmem, src=x_hbm[p]):
                pltpu.sync_copy(src.at[i_vmem.at[0]], o_vmem)

            pltpu.emit_pipeline(
                body,
                grid=(n // SC_WINDOW,),
                in_specs=[pl.BlockSpec((1, SC_WINDOW), lambda i: (0, i))],
                out_specs=[pl.BlockSpec((SC_WINDOW, width), lambda i: (i, 0))],
                core_axis_name=("core", "subcore"),
                dimension_semantics=(pltpu.PARALLEL,),
            )(i_hbm, o_hbm[p])

    return gather_rows(*slabs, idx.reshape(1, n))


def _combine_kernel(*refs):
    y_refs, (route_ref, h_ref, g_ref, o_ref) = refs[:ROW_SPLIT], refs[ROW_SPLIT:]
    per_token = route_ref[...].T
    gate0 = ROUTE_ROWS.index("gate0")
    f = _unpack_bf16_pairs(_load_slabs(y_refs, (0,))) * per_token[:, gate0:gate0 + 1]
    for k in range(1, TOP_K):
        f = f + _unpack_bf16_pairs(_load_slabs(y_refs, (k,))) * per_token[:, gate0 + k:gate0 + k + 1]
    o_ref[...] = h_ref[...] + _rms(f, g_ref[...])


def _combine(y_slabs, record, h, g_post):
    t = h.shape[0]
    tm = min(ROW_TILE, t)
    return pl.pallas_call(
        _combine_kernel,
        grid=(t // tm,),
        in_specs=[pl.BlockSpec((TOP_K, tm, SLAB), lambda i: (0, i, 0))] * ROW_SPLIT + [
            pl.BlockSpec((SUBLANES, tm), lambda i: (0, i)),
            pl.BlockSpec((tm, D_MODEL), lambda i: (i, 0)),
            pl.BlockSpec((1, D_MODEL), lambda i: (0, 0)),
        ],
        out_specs=pl.BlockSpec((tm, D_MODEL), lambda i: (i, 0)),
        out_shape=jax.ShapeDtypeStruct(h.shape, F32),
        input_output_aliases={ROW_SPLIT + 1: 0},
        compiler_params=_params(("parallel",)),
        name="moe_combine",
    )(*y_slabs, record, h, g_post.reshape(1, D_MODEL))


def _route(record, counts, bm):
    t = record.shape[1]
    rows = {name: record[pos] for pos, name in enumerate(ROUTE_ROWS)}
    counts = counts[:N_EXPERTS, 0].astype(jnp.int32)
    padded = (counts + bm - 1) // bm * bm
    pad_end = jnp.cumsum(padded)
    pad_start = pad_end - padded
    dest = []
    for k in range(TOP_K):
        choice = rows[f"choice{k}"].astype(jnp.int32)
        start = sum(jnp.where(choice == e, pad_start[e], 0) for e in range(N_EXPERTS))
        dest.append(start + rows[f"rank{k}"].astype(jnp.int32))
    dest = jnp.stack(dest, axis=0)
    n_blocks = (t * TOP_K) // bm + N_EXPERTS
    n_used = (pad_end[-1] // bm).astype(jnp.int32)
    blk = jnp.arange(n_blocks, dtype=jnp.int32)
    last = jnp.minimum(blk, n_used - 1) * bm
    block_expert = jnp.sum((last[:, None] >= pad_end[None, :]).astype(jnp.int32), axis=1)
    block_expert = jnp.minimum(block_expert, N_EXPERTS - 1)
    n_valid = jnp.clip(pad_start[block_expert] + counts[block_expert] - blk * bm, 0, bm)
    n_valid = jnp.where(blk < n_used, n_valid, 0).astype(jnp.int32)
    return dest, block_expert, n_used.reshape(1), n_valid, n_blocks * bm


def kernel(x, mix_norm_pre, mix_norm_post, ffn_norm_pre, ffn_norm_post, w_in, pool_w, pool_scale, na_rpb,
           w_out, dense_w_gate, dense_w_up, dense_w_down, moe_router, moe_w_gate, moe_w_up, moe_w_down):
    b, s, d = x.shape
    assert d == D_MODEL and s % GRID_W == 0
    t = b * s
    rows = s // GRID_W
    depth = w_in.shape[0]
    h = x.reshape(t, d)
    moe_w = {i: [_sc_cast_pairs(w[i]) for w in (moe_w_gate, moe_w_up, moe_w_down)]
             for i in range(moe_w_gate.shape[0])}
    for layer in range(depth):
        j = layer // 2
        is_moe = layer % 2 == 1
        proj = _norm_proj(h, mix_norm_pre[layer], w_in[layer].astype(BF16))
        y_na = _neighbourhood_attention(proj.reshape(b, rows, GRID_W, D_IN), na_rpb[layer])
        router = None
        if is_moe:
            router = jnp.pad(moe_router[j].T, ((0, EXPERT_ROWS - N_EXPERTS), (0, 0))).astype(BF16)
        outs = _out_proj(proj, s, pool_w[layer].astype(BF16), pool_scale[layer], y_na.reshape(t, D_ATTN),
                         w_out[layer].astype(BF16), h, mix_norm_post[layer], ffn_norm_pre[layer], router)
        if is_moe:
            h1, record, counts = outs[:3]
            dest, block_expert, n_used, n_valid, n_rows = _route(record, counts, min(FFN_ROWS, t))
            xs = _sc_dispatch(list(outs[3:]), dest, n_rows)
            ys = _swiglu(xs, *moe_w[j], block_expert, n_used, n_valid)
            by_choice = _sc_gather(list(ys), dest.reshape(TOP_K * t))
            h = _combine([y.reshape(TOP_K, t, SLAB) for y in by_choice], record, h1, ffn_norm_post[layer])
        else:
            (h1,) = outs
            bm = min(FFN_ROWS, t)
            n_blk = t // bm
            h = _swiglu(h1, dense_w_gate[j][None].astype(BF16), dense_w_up[j][None].astype(BF16),
                        dense_w_down[j][None].astype(BF16), jnp.zeros((n_blk,), jnp.int32),
                        jnp.full((1,), n_blk, jnp.int32), jnp.full((n_blk,), bm, jnp.int32),
                        g_pre=ffn_norm_pre[layer], g_post=ffn_norm_post[layer])
    return h.reshape(b, s, d)
```

```python
import functools

import jax
import jax.numpy as jnp
import numpy as np
from jax import lax
from jax.experimental import pallas as pl
from jax.experimental.pallas import tpu as pltpu
from jax.experimental.pallas import tpu_sc as plsc

F32 = jnp.float32
BF16 = jnp.bfloat16

D_MODEL = 1024
D_POOL = 512
POOL_WINDOWS = (2, 4, 8, 16)
POOL_GROUP_DIM = 128
D_ATTN = 512
NA_HEAD_DIM = 32
NA_HEADS = 16
D_IN = D_POOL + 3 * D_ATTN
GRID_W = 64
NA_KH = 8
NA_KW = 16
N_EXPERTS = 8
TOP_K = 2
RMS_EPS = 1e-6
NEG_BIG = -1e30
LOG2E = 1.4426950408889634

LANES = 128
SUBLANES = 8
HEADS_PER_GROUP = 256 // NA_HEAD_DIM
VMEM_LIMIT = 52 * 1024 * 1024

ROW_TILE = 1024
POOL_SUB = 128
POOL_HALO = 64
FFN_ROWS = 512
FFN_COLS_MAX = 2048
DENSE_ROWS = 1024
DENSE_COLS_MAX = 1024
FFN_CHUNK_ROWS = 256
MXU_TILE = 256
NA_ROWS_PER_STEP = 4
NA_SLOTS = 8
SC_WINDOW = 128
ROW_SPLIT = 2


def _rms(x, g):
    ms = jnp.mean(x * x, axis=-1, keepdims=True)
    return x * lax.rsqrt(ms + RMS_EPS) * g


def _params(sem):
    return pltpu.CompilerParams(dimension_semantics=sem, vmem_limit_bytes=VMEM_LIMIT)


def _norm_proj_kernel(x_ref, g_ref, w_ref, o_ref):
    hn = _rms(x_ref[...], g_ref[...])
    p = jnp.dot(hn.astype(BF16), w_ref[...], preferred_element_type=F32)
    q_lo, q_hi = D_POOL, D_POOL + D_ATTN
    o_ref[:, :q_lo] = p[:, :q_lo].astype(BF16)
    o_ref[:, q_lo:q_hi] = (p[:, q_lo:q_hi] * (NA_HEAD_DIM ** -0.5 * LOG2E)).astype(BF16)
    o_ref[:, q_hi:] = p[:, q_hi:].astype(BF16)


def _norm_proj(h, g, w_bf16):
    t = h.shape[0]
    tm = min(ROW_TILE, t)
    return pl.pallas_call(
        _norm_proj_kernel,
        grid=(t // tm,),
        in_specs=[
            pl.BlockSpec((tm, D_MODEL), lambda i: (i, 0)),
            pl.BlockSpec((1, D_MODEL), lambda i: (0, 0)),
            pl.BlockSpec((D_MODEL, D_IN), lambda i: (0, 0)),
        ],
        out_specs=pl.BlockSpec((tm, D_IN), lambda i: (i, 0)),
        out_shape=jax.ShapeDtypeStruct((t, D_IN), BF16),
        compiler_params=_params(("parallel",)),
        name="norm_proj",
    )(h, g.reshape(1, D_MODEL), w_bf16)


def _pool_bands():
    kdim = POOL_SUB + 2 * POOL_HALO
    rel = np.arange(kdim)[None, :] - np.arange(POOL_SUB)[:, None] - POOL_HALO
    return np.stack([((rel >= -(w // 2)) & (rel < w - w // 2)) for w in POOL_WINDOWS]).astype(np.float32)


def _pool_tile(seq_len, cur_ref, prev_ref, next_ref, band_ref, pw_ref, ps_ref, win_ref):
    tm = cur_ref.shape[0]
    tiles_per_seq = seq_len // tm
    i = pl.program_id(0) % tiles_per_seq
    t0 = i * tm
    halo_zero = jnp.zeros((POOL_HALO, D_POOL), BF16)
    win_ref[:POOL_HALO, :] = jnp.where(i == 0, halo_zero, prev_ref[...])
    win_ref[POOL_HALO:POOL_HALO + tm, :] = cur_ref[...]
    win_ref[POOL_HALO + tm:, :] = jnp.where(i == tiles_per_seq - 1, halo_zero, next_ref[...])
    kdim = POOL_SUB + 2 * POOL_HALO
    t_abs = t0 + lax.broadcasted_iota(jnp.int32, (tm, 1), 0)
    groups = []
    for g, w in enumerate(POOL_WINDOWS):
        half = w // 2
        c0, c1 = g * POOL_GROUP_DIM, (g + 1) * POOL_GROUP_DIM
        cnt = (jnp.minimum(t_abs + (w - half), seq_len) - jnp.maximum(t_abs - half, 0)).astype(F32)
        wsum = jnp.concatenate(
            [jnp.dot(band_ref[g], win_ref[s * POOL_SUB:s * POOL_SUB + kdim, c0:c1], preferred_element_type=F32)
             for s in range(tm // POOL_SUB)], axis=0)
        delta = wsum / cnt - cur_ref[:, c0:c1].astype(F32)
        y = jnp.dot(delta.astype(BF16), pw_ref[g], preferred_element_type=F32)
        groups.append((y * ps_ref[:, c0:c1]).astype(BF16))
    return jnp.concatenate(groups, axis=1)


RPB_SHIFT = GRID_W - NA_KW


def _na_build_bias(rpb_ref, bias_ref):
    n_dr = rpb_ref.shape[1]
    c = lax.broadcasted_iota(jnp.int32, (GRID_W, 2 * GRID_W), 0)
    lane = lax.broadcasted_iota(jnp.int32, (GRID_W, 2 * GRID_W), 1)
    kc = lane % GRID_W
    c0 = jnp.clip(c - NA_KW // 2, 0, GRID_W - NA_KW)
    inside = (kc >= c0) & (kc < c0 + NA_KW)
    first_half = lane < GRID_W

    def per_head(h, carry):
        halves = []
        for dr in range(n_dr):
            row = jnp.broadcast_to(rpb_ref[h, dr:dr + 1, :], (GRID_W, 2 * GRID_W)) * LOG2E
            halves.append((pltpu.roll(row, GRID_W + 1, 1, stride=1, stride_axis=0),
                           pltpu.roll(row, 1, 1, stride=1, stride_axis=0)))
        for e in range(n_dr - 1):
            pair = jnp.where(first_half, halves[e][0], halves[e + 1][1])
            bias_ref[h, e] = jnp.where(inside, pair, NEG_BIG)
        return carry

    lax.fori_loop(0, rpb_ref.shape[0], per_head, 0)


def _na_kernel(n_rows, q_ref, k_ref, v_ref, rpb_ref, o_ref, s_ref, p_ref, bias_ref):
    @pl.when(pl.program_id(0) == 0)
    def _():
        _na_build_bias(rpb_ref, bias_ref)

    n_batch, rows_per_step = q_ref.shape[0], q_ref.shape[1]
    win_rows = k_ref.shape[1]
    gw = HEADS_PER_GROUP * NA_HEAD_DIM
    n_groups = NA_HEADS // HEADS_PER_GROUP
    n_slots = s_ref.shape[0]
    lane_head = lax.broadcasted_iota(jnp.int32, (HEADS_PER_GROUP, 1, gw), 2) // NA_HEAD_DIM
    head_id = lax.broadcasted_iota(jnp.int32, (HEADS_PER_GROUP, 1, gw), 0)
    own = lane_head == head_id
    out_head = lax.broadcasted_iota(jnp.int32, (GRID_W, gw), 1) // NA_HEAD_DIM
    first_row = pl.program_id(0) * rows_per_step
    win0 = jnp.clip(first_row - NA_KH // 2, 0, n_rows - win_rows)
    chain = 0
    for j in range(rows_per_step):
        r = first_row + j
        r0 = jnp.clip(r - NA_KH // 2, 0, n_rows - NA_KH)
        off = r0 - win0
        d = r0 - r + NA_KH - 1
        for bi in range(n_batch):
            for g in range(n_groups):
                slot = chain % n_slots
                chain += 1
                lo, hi = g * gw, (g + 1) * gw
                qg = q_ref[bi, j, :, lo:hi]
                zero = jnp.zeros_like(qg)
                qm = jnp.where(own, qg[None], zero[None]).reshape(HEADS_PER_GROUP * GRID_W, gw)
                kw = k_ref[bi, pl.ds(off, NA_KH), :, lo:hi].reshape(NA_KH * GRID_W, gw)
                vw = v_ref[bi, pl.ds(off, NA_KH), :, lo:hi].reshape(NA_KH * GRID_W, gw)
                s_ref[slot] = lax.dot_general(qm, kw, (((1,), (1,)), ((), ())), preferred_element_type=F32)
                inv_l = []
                for h in range(HEADS_PER_GROUP):
                    rows = slice(h * GRID_W, (h + 1) * GRID_W)
                    bias = jnp.concatenate([bias_ref[g * HEADS_PER_GROUP + h, d + 2 * pair]
                                            for pair in range(NA_KH // 2)], axis=1)
                    s = s_ref[slot, rows, :] + bias
                    p = jnp.exp2(s - jnp.max(s, axis=-1, keepdims=True))
                    inv_l.append(1.0 / jnp.sum(p, axis=-1, keepdims=True))
                    p_ref[slot, rows, :] = p.astype(BF16)
                pv = jnp.dot(p_ref[slot], vw, preferred_element_type=F32)
                out = pv[:GRID_W] * inv_l[0]
                for h in range(1, HEADS_PER_GROUP):
                    out = jnp.where(out_head == h, pv[h * GRID_W:(h + 1) * GRID_W] * inv_l[h], out)
                o_ref[bi, j, :, lo:hi] = out.astype(BF16)


def _neighbourhood_attention(proj4, rpb):
    b, rows, w, _ = proj4.shape
    assert w == GRID_W and rows >= NA_KH
    n_heads, n_dr, n_dc = rpb.shape
    rpb_rows = jnp.pad(rpb.astype(F32), ((0, 0), (0, 0), (RPB_SHIFT, 2 * GRID_W - RPB_SHIFT - n_dc)))
    rps = next(c for c in (NA_ROWS_PER_STEP, 2, 1) if rows % c == 0 and rows >= c + NA_KH - 1)
    win_rows = rps + NA_KH - 1
    n_stack = HEADS_PER_GROUP * GRID_W
    n_keys = NA_KH * GRID_W

    def window(col_block):
        shape = (pl.Element(b), pl.Element(win_rows), pl.Element(GRID_W), pl.Element(D_ATTN))
        return pl.BlockSpec(
            shape, lambda i: (0, jnp.clip(i * rps - NA_KH // 2, 0, rows - win_rows), 0, col_block * D_ATTN))

    return pl.pallas_call(
        functools.partial(_na_kernel, rows),
        grid=(rows // rps,),
        in_specs=[pl.BlockSpec((b, rps, GRID_W, D_ATTN), lambda i: (0, i, 0, 1)), window(2), window(3),
                  pl.BlockSpec(rpb_rows.shape, lambda i: (0, 0, 0))],
        out_specs=pl.BlockSpec((b, rps, GRID_W, D_ATTN), lambda i: (0, i, 0, 0)),
        out_shape=jax.ShapeDtypeStruct((b, rows, GRID_W, D_ATTN), BF16),
        scratch_shapes=[pltpu.VMEM((NA_SLOTS, n_stack, n_keys), F32),
                        pltpu.VMEM((NA_SLOTS, n_stack, n_keys), BF16),
                        pltpu.VMEM((n_heads, n_dr - 1, GRID_W, 2 * GRID_W), F32)],
        compiler_params=_params(("arbitrary",)),
        name="neighbourhood_attention",
    )(proj4, proj4, proj4, rpb_rows)


def _pack_bf16_pairs(x):
    n = x.shape[1] // 2
    hi = lax.bitcast_convert_type(x[:, :n].astype(BF16).astype(F32), jnp.uint32)
    lo = lax.bitcast_convert_type(x[:, n:].astype(BF16).astype(F32), jnp.uint32)
    return hi | (lo >> 16)


def _unpack_bf16_pairs(w):
    hi = lax.bitcast_convert_type(w & jnp.uint32(0xFFFF0000), F32)
    lo = lax.bitcast_convert_type(w << 16, F32)
    return jnp.concatenate([hi, lo], axis=1)


SLAB = D_MODEL // 2 // ROW_SPLIT


def _store_slabs(slab_refs, words):
    for p, ref in enumerate(slab_refs):
        ref[...] = words[:, p * SLAB:(p + 1) * SLAB]


def _load_slabs(slab_refs, lead=()):
    return jnp.concatenate([ref[lead + (slice(None), slice(None))] for ref in slab_refs], axis=1)


ROUTE_ROWS = ("choice0", "choice1", "gate0", "gate1", "rank0", "rank1")
EXPERT_ROWS = 16


def _top2_route(logits_t, earlier_ref, carry_ref):
    eid = lax.broadcasted_iota(jnp.int32, logits_t.shape, 0).astype(F32)
    valid = jnp.where(eid < N_EXPERTS, logits_t, -jnp.inf)
    top = []
    for _ in range(TOP_K):
        m = jnp.max(valid, axis=0, keepdims=True)
        idx = jnp.min(jnp.where(valid == m, eid, float(EXPERT_ROWS)), axis=0, keepdims=True)
        top.append((m, idx))
        valid = jnp.where(eid == idx, -jnp.inf, valid)
    (m0, i0), (m1, i1) = top
    e = jnp.exp(m1 - m0)
    g0 = 1.0 / (1.0 + e)
    g1 = e * g0
    hot0 = (eid == i0).astype(F32)
    hot1 = (eid == i1).astype(F32)
    picked = hot0 + hot1
    carry = carry_ref[:, 0:1]
    earlier = jnp.dot(picked.astype(BF16), earlier_ref[...], preferred_element_type=F32) + carry
    r0 = jnp.sum(earlier * hot0, axis=0, keepdims=True)
    r1 = jnp.sum(earlier * hot1, axis=0, keepdims=True)
    row = lax.broadcasted_iota(jnp.int32, (SUBLANES, logits_t.shape[1]), 0)
    record = jnp.zeros((SUBLANES, logits_t.shape[1]), F32)
    for pos, val in enumerate((i0, i1, g0, g1, r0, r1)):
        record = jnp.where(row == pos, val, record)
    return record, carry + jnp.sum(picked, axis=1, keepdims=True)


def _out_proj_kernel(with_router, seq_len, cur_ref, prev_ref, next_ref, band_ref, pw_ref, ps_ref,
                     ya_ref, w_ref, h_ref, gpost_ref, gpre_ref, *refs):
    win_ref, refs = refs[-1], refs[:-1]
    y_pool = _pool_tile(seq_len, cur_ref, prev_ref, next_ref, band_ref, pw_ref, ps_ref, win_ref)
    mix = jnp.dot(y_pool, w_ref[:D_POOL, :], preferred_element_type=F32)
    mix = mix + jnp.dot(ya_ref[...], w_ref[D_POOL:, :], preferred_element_type=F32)
    h1 = h_ref[...] + _rms(mix, gpost_ref[...])
    if with_router:
        hn = _rms(h1, gpre_ref[...])
        wr_ref, earlier_ref, h1_ref, route_ref, counts_ref = refs[:5]
        slab_refs, carry_ref = refs[5:-1], refs[-1]
        _store_slabs(slab_refs, _pack_bf16_pairs(hn))

        @pl.when(pl.program_id(0) == 0)
        def _():
            carry_ref[...] = jnp.zeros_like(carry_ref)

        logits_t = lax.dot_general(wr_ref[...], hn.astype(BF16), (((1,), (1,)), ((), ())),
                                   preferred_element_type=F32)
        record, counts = _top2_route(logits_t, earlier_ref, carry_ref)
        route_ref[...] = record
        carry_ref[...] = jnp.broadcast_to(counts, carry_ref.shape)
        counts_ref[...] = jnp.broadcast_to(counts, counts_ref.shape)
    else:
        (h1_ref,) = refs
    h1_ref[...] = h1


def _out_proj(proj, seq_len, pool_w_bf16, pool_scale, y_na, w_out_bf16, h, g_post, g_pre, router_bf16=None):
    t = h.shape[0]
    tm = min(ROW_TILE, seq_len)
    assert seq_len % tm == 0 and tm % POOL_HALO == 0
    with_router = router_bf16 is not None
    row = lambda width: pl.BlockSpec((tm, width), lambda i: (i, 0))
    full = lambda *shape: pl.BlockSpec(shape, lambda i: (0,) * len(shape))
    hb = tm // POOL_HALO
    n_halo = t // POOL_HALO
    bands = jnp.asarray(_pool_bands(), BF16)
    halo_specs = [pl.BlockSpec((POOL_HALO, D_POOL), lambda i: (jnp.maximum(i * hb - 1, 0), 0)),
                  pl.BlockSpec((POOL_HALO, D_POOL), lambda i: (jnp.minimum((i + 1) * hb, n_halo - 1), 0))]
    in_specs = [row(D_POOL)] + halo_specs + [full(*bands.shape), full(*pool_w_bf16.shape), full(1, D_POOL)]
    in_specs += [row(D_ATTN), full(D_MODEL, D_MODEL), row(D_MODEL), full(1, D_MODEL), full(1, D_MODEL)]
    args = [proj, proj, proj, bands, pool_w_bf16, pool_scale.reshape(1, D_POOL),
            y_na, w_out_bf16, h, g_post.reshape(1, D_MODEL), g_pre.reshape(1, D_MODEL)]
    scratch = []
    if with_router:
        earlier_tokens = jnp.asarray(np.triu(np.ones((tm, tm), np.float32), 1), BF16)
        in_specs += [full(EXPERT_ROWS, D_MODEL), full(tm, tm)]
        args += [router_bf16, earlier_tokens]
        out_specs = [row(D_MODEL), pl.BlockSpec((SUBLANES, tm), lambda i: (0, i)), full(EXPERT_ROWS, LANES)]
        out_specs += [row(SLAB)] * ROW_SPLIT
        out_shape = [jax.ShapeDtypeStruct((t, D_MODEL), F32), jax.ShapeDtypeStruct((SUBLANES, t), F32),
                     jax.ShapeDtypeStruct((EXPERT_ROWS, LANES), F32)]
        out_shape += [jax.ShapeDtypeStruct((t, SLAB), jnp.uint32)] * ROW_SPLIT
        scratch = [pltpu.VMEM((EXPERT_ROWS, LANES), F32)]
    else:
        out_specs = [row(D_MODEL)]
        out_shape = [jax.ShapeDtypeStruct((t, D_MODEL), F32)]
    return pl.pallas_call(
        functools.partial(_out_proj_kernel, with_router, seq_len),
        grid=(t // tm,),
        in_specs=in_specs,
        out_specs=out_specs,
        out_shape=out_shape,
        scratch_shapes=scratch + [pltpu.VMEM((tm + 2 * POOL_HALO, D_POOL), BF16)],
        compiler_params=_params(("arbitrary",) if with_router else ("parallel",)),
        name="out_proj_router" if with_router else "out_proj",
    )(*args)


def _swiglu_kernel(dense, be_ref, nused_ref, nvalid_ref, *refs):
    if dense:
        wg_ref, wu_ref, wd_ref, h_ref, gpre_ref, g_ref, o_ref, acc_ref = refs
    else:
        x_refs, (wg_ref, wu_ref, wd_ref) = refs[:ROW_SPLIT], refs[ROW_SPLIT:ROW_SPLIT + 3]
        o_refs, acc_ref = refs[ROW_SPLIT + 3:-1], refs[-1]
    j = pl.program_id(0)
    f = pl.program_id(1)
    live = j < nused_ref[0]

    @pl.when((j == 0) & (f == 0))
    def _():
        acc_ref[...] = jnp.zeros_like(acc_ref)

    @pl.when(live)
    def _():
        if dense:
            x = _rms(h_ref[...], gpre_ref[...]).astype(BF16)
        else:
            row = lax.broadcasted_iota(jnp.int32, (acc_ref.shape[0], 1), 0)
            words = jnp.where(row < nvalid_ref[j], _load_slabs(x_refs), jnp.uint32(0))
            x = _unpack_bf16_pairs(words).astype(BF16)
        chunk = min(FFN_CHUNK_ROWS, x.shape[0])
        for c in range(x.shape[0] // chunk):
            rows = slice(c * chunk, (c + 1) * chunk)
            a = jnp.dot(x[rows], wg_ref[0], preferred_element_type=F32)
            b = jnp.dot(x[rows], wu_ref[0], preferred_element_type=F32)
            hmid = (a * jax.nn.sigmoid(a) * b).astype(BF16)
            part = jnp.dot(hmid, wd_ref[0], preferred_element_type=F32)
            acc_ref[rows, :] = jnp.where(f == 0, part, acc_ref[rows, :] + part)

    @pl.when(f == pl.num_programs(1) - 1)
    def _():
        @pl.when(live)
        def _():
            if dense:
                o_ref[...] = h_ref[...] + _rms(acc_ref[...], g_ref[...])
            else:
                _store_slabs(o_refs, _pack_bf16_pairs(acc_ref[...]))

        @pl.when(jnp.logical_not(live))
        def _():
            for ref in ([o_ref] if dense else o_refs):
                ref[...] = jnp.zeros_like(ref)


def _swiglu(x, wg, wu, wd, block_expert, n_used, n_valid, g_pre=None, g_post=None):
    dense = g_pre is not None
    xs = [] if dense else list(x)
    rows = x.shape[0] if dense else xs[0].shape[0]
    dff = wg.shape[-1]
    bm = min(DENSE_ROWS if dense else FFN_ROWS, rows)
    cols_max = DENSE_COLS_MAX if dense else FFN_COLS_MAX
    tf = max(c for c in range(MXU_TILE, min(cols_max, dff) + 1, MXU_TILE) if dff % c == 0)
    assert rows % bm == 0 and dff % tf == 0
    nf = dff // tf

    def f_eff(j, f, nu):
        return jnp.where(j < nu[0], f, nf - 1)

    row_spec = lambda width: pl.BlockSpec((bm, width), lambda j, f, be, nu, nv: (j, 0))
    in_specs = [row_spec(a.shape[1]) for a in xs] + [
        pl.BlockSpec((1, D_MODEL, tf), lambda j, f, be, nu, nv: (be[j], 0, f_eff(j, f, nu))),
        pl.BlockSpec((1, D_MODEL, tf), lambda j, f, be, nu, nv: (be[j], 0, f_eff(j, f, nu))),
        pl.BlockSpec((1, tf, D_MODEL), lambda j, f, be, nu, nv: (be[j], f_eff(j, f, nu), 0)),
    ]
    args = xs + [wg, wu, wd]
    if dense:
        gain_spec = pl.BlockSpec((1, D_MODEL), lambda j, f, be, nu, nv: (0, 0))
        in_specs += [row_spec(D_MODEL), gain_spec, gain_spec]
        args += [x, g_pre.reshape(1, D_MODEL), g_post.reshape(1, D_MODEL)]
        out_shape, out_specs = jax.ShapeDtypeStruct((rows, D_MODEL), F32), row_spec(D_MODEL)
    else:
        out_shape = [jax.ShapeDtypeStruct((rows, SLAB), jnp.uint32)] * ROW_SPLIT
        out_specs = [row_spec(SLAB)] * ROW_SPLIT
    return pl.pallas_call(
        functools.partial(_swiglu_kernel, dense),
        grid_spec=pltpu.PrefetchScalarGridSpec(
            num_scalar_prefetch=3,
            grid=(rows // bm, nf),
            in_specs=in_specs,
            out_specs=out_specs,
            scratch_shapes=[pltpu.VMEM((bm, D_MODEL), F32)],
        ),
        out_shape=out_shape,
        compiler_params=_params(("parallel", "arbitrary")),
        name="swiglu_dense" if dense else "swiglu_experts",
    )(block_expert, n_used, n_valid, *args)


def _sc_mesh():
    return plsc.VectorSubcoreMesh(core_axis_name="core", subcore_axis_name="subcore")


def _sc_dispatch(slabs, dest, n_rows):
    t, width = slabs[0].shape
    idx = [dest[k].reshape(1, t) for k in range(TOP_K)]
    n_slabs = len(slabs)

    @functools.partial(pl.kernel, mesh=_sc_mesh(), scratch_types=[], name="moe_dispatch_sc",
                       out_type=[jax.ShapeDtypeStruct((n_rows, width), slabs[0].dtype)] * n_slabs)
    def scatter_rows(*refs):
        x_hbm, idx_hbm, o_hbm = refs[:n_slabs], refs[n_slabs:n_slabs + TOP_K], refs[n_slabs + TOP_K:]
        for p in range(n_slabs):
            def body(x_vmem, *idx_vmem, out=o_hbm[p]):
                for i_vmem in idx_vmem:
                    pltpu.sync_copy(x_vmem, out.at[i_vmem.at[0]])

            pltpu.emit_pipeline(
                body,
                grid=(t // SC_WINDOW,),
                in_specs=[pl.BlockSpec((SC_WINDOW, width), lambda i: (i, 0))]
                + [pl.BlockSpec((1, SC_WINDOW), lambda i: (0, i))] * TOP_K,
                out_specs=[],
                core_axis_name=("core", "subcore"),
                dimension_semantics=(pltpu.PARALLEL,),
            )(x_hbm[p], *idx_hbm)

    return scatter_rows(*slabs, *idx)


def _sc_gather(slabs, idx):
    n = idx.shape[0]
    width = slabs[0].shape[1]
    n_slabs = len(slabs)

    @functools.partial(pl.kernel, mesh=_sc_mesh(), scratch_types=[], name="moe_gather_sc",
                       out_type=[jax.ShapeDtypeStruct((n, width), slabs[0].dtype)] * n_slabs)
    def gather_rows(*refs):
        x_hbm, i_hbm, o_hbm = refs[:n_slabs], refs[n_slabs], refs[n_slabs + 1:]
        for p in range(n_slabs):
            def body(i_vmem, o_vmem, src=x_hbm[p]):
                pltpu.sync_copy(src.at[i_vmem.at[0]], o_vmem)

            pltpu.emit_pipeline(
                body,
                grid=(n // SC_WINDOW,),
                in_specs=[pl.BlockSpec((1, SC_WINDOW), lambda i: (0, i))],
                out_specs=[pl.BlockSpec((SC_WINDOW, width), lambda i: (i, 0))],
                core_axis_name=("core", "subcore"),
                dimension_semantics=(pltpu.PARALLEL,),
            )(i_hbm, o_hbm[p])

    return gather_rows(*slabs, idx.reshape(1, n))


def _combine_kernel(*refs):
    y_refs, (route_ref, h_ref, g_ref, o_ref) = refs[:ROW_SPLIT], refs[ROW_SPLIT:]
    per_token = route_ref[...].T
    gate0 = ROUTE_ROWS.index("gate0")
    f = _unpack_bf16_pairs(_load_slabs(y_refs, (0,))) * per_token[:, gate0:gate0 + 1]
    for k in range(1, TOP_K):
        f = f + _unpack_bf16_pairs(_load_slabs(y_refs, (k,))) * per_token[:, gate0 + k:gate0 + k + 1]
    o_ref[...] = h_ref[...] + _rms(f, g_ref[...])


def _combine(y_slabs, record, h, g_post):
    t = h.shape[0]
    tm = min(ROW_TILE, t)
    return pl.pallas_call(
        _combine_kernel,
        grid=(t // tm,),
        in_specs=[pl.BlockSpec((TOP_K, tm, SLAB), lambda i: (0, i, 0))] * ROW_SPLIT + [
            pl.BlockSpec((SUBLANES, tm), lambda i: (0, i)),
            pl.BlockSpec((tm, D_MODEL), lambda i: (i, 0)),
            pl.BlockSpec((1, D_MODEL), lambda i: (0, 0)),
        ],
        out_specs=pl.BlockSpec((tm, D_MODEL), lambda i: (i, 0)),
        out_shape=jax.ShapeDtypeStruct(h.shape, F32),
        input_output_aliases={ROW_SPLIT + 1: 0},
        compiler_params=_params(("parallel",)),
        name="moe_combine",
    )(*y_slabs, record, h, g_post.reshape(1, D_MODEL))


def _route(record, counts, bm):
    t = record.shape[1]
    rows = {name: record[pos] for pos, name in enumerate(ROUTE_ROWS)}
    counts = counts[:N_EXPERTS, 0].astype(jnp.int32)
    padded = (counts + bm - 1) // bm * bm
    pad_end = jnp.cumsum(padded)
    pad_start = pad_end - padded
    dest = []
    for k in range(TOP_K):
        choice = rows[f"choice{k}"].astype(jnp.int32)
        start = sum(jnp.where(choice == e, pad_start[e], 0) for e in range(N_EXPERTS))
        dest.append(start + rows[f"rank{k}"].astype(jnp.int32))
    dest = jnp.stack(dest, axis=0)
    n_blocks = (t * TOP_K) // bm + N_EXPERTS
    n_used = (pad_end[-1] // bm).astype(jnp.int32)
    blk = jnp.arange(n_blocks, dtype=jnp.int32)
    last = jnp.minimum(blk, n_used - 1) * bm
    block_expert = jnp.sum((last[:, None] >= pad_end[None, :]).astype(jnp.int32), axis=1)
    block_expert = jnp.minimum(block_expert, N_EXPERTS - 1)
    n_valid = jnp.clip(pad_start[block_expert] + counts[block_expert] - blk * bm, 0, bm)
    n_valid = jnp.where(blk < n_used, n_valid, 0).astype(jnp.int32)
    return dest, block_expert, n_used.reshape(1), n_valid, n_blocks * bm


def kernel(x, mix_norm_pre, mix_norm_post, ffn_norm_pre, ffn_norm_post, w_in, pool_w, pool_scale, na_rpb,
           w_out, dense_w_gate, dense_w_up, dense_w_down, moe_router, moe_w_gate, moe_w_up, moe_w_down):
    b, s, d = x.shape
    assert d == D_MODEL and s % GRID_W == 0
    t = b * s
    rows = s // GRID_W
    depth = w_in.shape[0]
    h = x.reshape(t, d)
    for layer in range(depth):
        j = layer // 2
        is_moe = layer % 2 == 1
        proj = _norm_proj(h, mix_norm_pre[layer], w_in[layer].astype(BF16))
        y_na = _neighbourhood_attention(proj.reshape(b, rows, GRID_W, D_IN), na_rpb[layer])
        router = None
        if is_moe:
            router = jnp.pad(moe_router[j].T, ((0, EXPERT_ROWS - N_EXPERTS), (0, 0))).astype(BF16)
        outs = _out_proj(proj, s, pool_w[layer].astype(BF16), pool_scale[layer], y_na.reshape(t, D_ATTN),
                         w_out[layer].astype(BF16), h, mix_norm_post[layer], ffn_norm_pre[layer], router)
        if is_moe:
            h1, record, counts = outs[:3]
            dest, block_expert, n_used, n_valid, n_rows = _route(record, counts, min(FFN_ROWS, t))
            xs = _sc_dispatch(list(outs[3:]), dest, n_rows)
            ys = _swiglu(xs, moe_w_gate[j].astype(BF16), moe_w_up[j].astype(BF16),
                         moe_w_down[j].astype(BF16), block_expert, n_used, n_valid)
            by_choice = _sc_gather(list(ys), dest.reshape(TOP_K * t))
            h = _combine([y.reshape(TOP_K, t, SLAB) for y in by_choice], record, h1, ffn_norm_post[layer])
        else:
            (h1,) = outs
            bm = min(DENSE_ROWS, t)
            n_blk = t // bm
            h = _swiglu(h1, dense_w_gate[j][None].astype(BF16), dense_w_up[j][None].astype(BF16),
                        dense_w_down[j][None].astype(BF16), jnp.zeros((n_blk,), jnp.int32),
                        jnp.full((1,), n_blk, jnp.int32), jnp.full((n_blk,), bm, jnp.int32),
                        g_pre=ffn_norm_pre[layer], g_post=ffn_norm_post[layer])
    return h.reshape(b, s, d)
```
